```python
import math
import jax
import jax.numpy as jnp
from jax import lax
import numpy as np

D_MODEL = 2048
BATCH = 1
SEQ = 8192
DEPTH = 4

HEAD_DIM = 128
ROPE_DIM = HEAD_DIM // 4
ROPE_THETA = 500000.0
Q_BLOCK = 128

NSA_HEADS = 6
NSA_KV_HEADS = 2
NSA_GROUP = NSA_HEADS // NSA_KV_HEADS
CMP_LEN = 32
CMP_STRIDE = 16
SLC_LEN = 64
SLC_TOPK = 16
WINDOW = 512
FORCE_BONUS = 1.0e4

GDN_HEADS = 6
GDN_CONV = 4
GDN_CHUNK = 64

SB_HEADS = 4

NSA_W = NSA_HEADS * HEAD_DIM
KV_W = NSA_KV_HEADS * HEAD_DIM
GDN_W = GDN_HEADS * HEAD_DIM
SB_W = SB_HEADS * HEAD_DIM
MIX_W = NSA_W + GDN_W + SB_W
IN_SPLITS = (NSA_W, KV_W, KV_W, KV_W, KV_W, KV_W, KV_W, 3 * NSA_HEADS, NSA_W,
             3 * GDN_W, GDN_HEADS, GDN_HEADS, GDN_W,
             SB_W, SB_W, SB_W, SB_W)
IN_COLS = sum(IN_SPLITS)

DEEPNORM_ALPHA = (2 * DEPTH) ** 0.25
DEEPNORM_BETA = (8 * DEPTH) ** -0.25
LN_EPS = 1e-5
RMS_EPS = 1e-6
NEG_INF = -1e30

kernel_name = 'hybrid_nsa_gdn_stickbreak_trunk'


def _layernorm(x, g, b):
    xf = x.astype(jnp.float32)
    mu = jnp.mean(xf, axis=-1, keepdims=True)
    var = jnp.mean(jnp.square(xf - mu), axis=-1, keepdims=True)
    return ((xf - mu) * lax.rsqrt(var + LN_EPS) * g + b).astype(x.dtype)


def _rope(x, pos):
    half = ROPE_DIM // 2
    inv = ROPE_THETA ** (-jnp.arange(half, dtype=jnp.float32) * 2.0 / ROPE_DIM)
    ang = pos.astype(jnp.float32)[..., None] * inv
    cos = jnp.cos(ang)[:, :, None, :].astype(x.dtype)
    sin = jnp.sin(ang)[:, :, None, :].astype(x.dtype)
    x1 = x[..., :half]
    x2 = x[..., half:ROPE_DIM]
    return jnp.concatenate([x1 * cos - x2 * sin, x2 * cos + x1 * sin, x[..., ROPE_DIM:]], axis=-1)


def _masked_softmax(s, mask, axis):
    s = jnp.where(mask, s, NEG_INF)
    m = jnp.max(s, axis=axis, keepdims=True)
    e = jnp.where(mask, jnp.exp(s - m), 0.0)
    return e / jnp.maximum(jnp.sum(e, axis=axis, keepdims=True), 1e-30)


def _to_blocks(x):
    b, s = x.shape[:2]
    return jnp.moveaxis(x.reshape((b, s // Q_BLOCK, Q_BLOCK) + x.shape[2:]), 1, 0)


def _from_blocks(y):
    n, b = y.shape[:2]
    return jnp.moveaxis(y, 0, 1).reshape((b, n * Q_BLOCK) + y.shape[3:])


def _split(p, sizes):
    cuts = np.cumsum(np.array(sizes))[:-1].tolist()
    return jnp.split(p, cuts, axis=-1)


def _nsa(q, gates, k_cmp, v_cmp, k_slc, v_slc, k_win, v_win, positions,
         pe_k, pe_v, w1_k, w2_k, w1_v, w2_v):
    b, s = q.shape[:2]
    dt = q.dtype
    n_cmp = (s - CMP_LEN) // CMP_STRIDE + 1
    n_slc = s // SLC_LEN
    n_top = min(SLC_TOPK, n_slc)
    scale = HEAD_DIM ** -0.5
    starts = jnp.arange(n_cmp) * CMP_STRIDE
    cmp_end = starts + CMP_LEN - 1
    cidx = starts[:, None] + jnp.arange(CMP_LEN)[None, :]

    def compress(kv, pe, w1, w2):
        blk = kv[:, cidx] + pe[None, None, :, None, :]
        flat = jnp.swapaxes(blk, 2, 3).reshape(b, n_cmp, NSA_KV_HEADS, CMP_LEN * HEAD_DIM)
        return jax.nn.silu(flat @ w1) @ w2

    kc = _rope(compress(k_cmp, pe_k, w1_k, w2_k), positions[:, cmp_end])
    vc = compress(v_cmp, pe_v, w1_v, w2_v)
    slc_ids = jnp.arange(n_slc)
    ov = (jnp.minimum(cmp_end[:, None] + 1, (slc_ids[None, :] + 1) * SLC_LEN)
          - jnp.maximum(starts[:, None], slc_ids[None, :] * SLC_LEN))
    overlap = jnp.maximum(ov, 0).astype(jnp.float32) / CMP_LEN
    kblk = jnp.moveaxis(k_slc.reshape(b, n_slc, SLC_LEN, NSA_KV_HEADS, HEAD_DIM), 3, 1)
    vblk = jnp.moveaxis(v_slc.reshape(b, n_slc, SLC_LEN, NSA_KV_HEADS, HEAD_DIM), 3, 1)
    pad = ((0, 0), (WINDOW, 0), (0, 0), (0, 0))
    kwp = jnp.pad(k_win, pad)
    vwp = jnp.pad(v_win, pad)
    gather = jax.vmap(jax.vmap(lambda tab, ix: tab[ix]))
    slc_off = jnp.arange(SLC_LEN)
    win_off = jnp.arange(WINDOW + Q_BLOCK) - WINDOW

    def block(args):
        qb, gb, i = args
        t = i * Q_BLOCK + jnp.arange(Q_BLOCK)
        sc = jnp.einsum('btgrd,bngd->bgrtn', qb, kc).astype(jnp.float32) * scale
        pc = _masked_softmax(sc, cmp_end[None, :] <= t[:, None], -1)
        o_c = jnp.einsum('bgrtn,bngd->btgrd', pc.astype(dt), vc)
        imp = jnp.einsum('bgrtn,nj->bgtj', pc, overlap)
        cur = t // SLC_LEN
        future = slc_ids[None, :] * SLC_LEN > t[:, None]
        forced = ((slc_ids[None, :] == 0) | (slc_ids[None, :] == cur[:, None])
                  | (slc_ids[None, :] == cur[:, None] - 1))
        imp = jnp.where(future, NEG_INF, jnp.where(forced, imp + FORCE_BONUS, imp))
        top_v, top_i = lax.top_k(imp, n_top)
        ok = top_v > 0.5 * NEG_INF
        ks = gather(kblk, top_i)
        vs = gather(vblk, top_i)
        ss = jnp.einsum('btgrd,bgtkld->bgrtkl', qb, ks).astype(jnp.float32) * scale
        tok = top_i[..., None] * SLC_LEN + slc_off
        ms = ok[..., None] & (tok <= t[None, None, :, None, None])
        ps = _masked_softmax(ss, ms[:, :, None], (-2, -1))
        o_s = jnp.einsum('bgrtkl,bgtkld->btgrd', ps.astype(dt), vs)
        kw = lax.dynamic_slice_in_dim(kwp, i * Q_BLOCK, WINDOW + Q_BLOCK, axis=1)
        vw = lax.dynamic_slice_in_dim(vwp, i * Q_BLOCK, WINDOW + Q_BLOCK, axis=1)
        spos = i * Q_BLOCK + win_off
        dpos = t[:, None] - spos[None, :]
        mw = (dpos >= 0) & (dpos < WINDOW) & (spos[None, :] >= 0)
        sw = jnp.einsum('btgrd,bsgd->bgrts', qb, kw).astype(jnp.float32) * scale
        pw = _masked_softmax(sw, mw, -1)
        o_w = jnp.einsum('bgrts,bsgd->btgrd', pw.astype(dt), vw)
        return gb[..., 0:1] * o_c + gb[..., 1:2] * o_s + gb[..., 2:3] * o_w

    qg = q.reshape(b, s, NSA_KV_HEADS, NSA_GROUP, HEAD_DIM)
    gg = gates.reshape(b, s, NSA_KV_HEADS, NSA_GROUP, 3)
    out = lax.map(block, (_to_blocks(qg), _to_blocks(gg), jnp.arange(s // Q_BLOCK)))
    return _from_blocks(out).reshape(b, s, NSA_W)


def _causal_conv(x, w):
    ch = x.shape[-1]
    return lax.conv_general_dilated(x, w[:, None, :], (1,), [(GDN_CONV - 1, 0)],
                                    dimension_numbers=('NWC', 'WIO', 'NWC'),
                                    feature_group_count=ch)


def _gated_delta(q, k, v, beta, g):
    b, s, h, d = q.shape
    nc = s // GDN_CHUNK

    def chunks(a):
        return jnp.moveaxis(a.reshape((b, nc, GDN_CHUNK, h) + a.shape[3:]), 3, 1)

    q = chunks(q) * (d ** -0.5)
    k = chunks(k)
    v = chunks(v)
    beta = chunks(beta)
    gc = jnp.cumsum(chunks(g), axis=-1)
    idx = jnp.arange(GDN_CHUNK)
    incl = idx[:, None] >= idx[None, :]
    strict = idx[:, None] > idx[None, :]
    diff = gc[..., :, None] - gc[..., None, :]
    decay = jnp.where(incl, jnp.exp(jnp.where(incl, diff, 0.0)), 0.0)
    kb = k * beta[..., None]
    lower = jnp.where(strict, jnp.einsum('bhncd,bhned->bhnce', kb, k) * decay, 0.0)
    tri = lower + jnp.eye(GDN_CHUNK, dtype=jnp.float32)
    rhs = jnp.concatenate([v * beta[..., None], kb * jnp.exp(gc)[..., None]], axis=-1)
    sol = lax.linalg.triangular_solve(tri, rhs, left_side=True, lower=True, unit_diagonal=True)
    u = sol[..., :d]
    w = sol[..., d:]
    intra = jnp.einsum('bhncd,bhned->bhnce', q, k) * decay
    g_last = gc[..., -1]
    k_tail = k * jnp.exp(g_last[..., None] - gc)[..., None]
    q_head = q * jnp.exp(gc)[..., None]

    def step(state, xs):
        u_i, w_i, qh_i, kt_i, intra_i, gl_i = xs
        v_new = u_i - jnp.einsum('bhcd,bhde->bhce', w_i, state)
        o = (jnp.einsum('bhcd,bhde->bhce', qh_i, state)
             + jnp.einsum('bhce,bhef->bhcf', intra_i, v_new))
        state = (state * jnp.exp(gl_i)[..., None, None]
                 + jnp.einsum('bhcd,bhce->bhde', kt_i, v_new))
        return state, o

    xs = tuple(jnp.moveaxis(a, 2, 0) for a in (u, w, q_head, k_tail, intra, g_last))
    _, o = lax.scan(step, jnp.zeros((b, h, d, d), jnp.float32), xs)
    return jnp.transpose(o, (1, 0, 3, 2, 4)).reshape(b, s, h, d)


def _stick_breaking(q, k, v):
    b, s = q.shape[:2]
    dt = q.dtype
    scale = HEAD_DIM ** -0.5
    kpos = jnp.arange(s)

    def block(args):
        qb, i = args
        t = i * Q_BLOCK + jnp.arange(Q_BLOCK)
        z = jnp.einsum('bthd,bshd->bhts', qb, k).astype(jnp.float32) * scale
        past = kpos[None, :] < t[:, None]
        log_beta = jax.nn.log_sigmoid(z)
        log_1m = jnp.where(past, jax.nn.log_sigmoid(-z), 0.0)
        after = lax.cumsum(log_1m, axis=3, reverse=True) - log_1m
        a = jnp.where(past, jnp.exp(log_beta + after), 0.0)
        return jnp.einsum('bhts,bshd->bthd', a.astype(dt), v)

    out = lax.map(block, (_to_blocks(q), jnp.arange(s // Q_BLOCK)))
    return _from_blocks(out)


def setup_inputs(seed: int = 0) -> dict:
    key = jax.random.key(seed)
    ks = jax.random.split(key, 20)
    f32 = jnp.float32

    def nrm(k, shape, std):
        return jax.random.normal(k, shape, f32) * std

    x = nrm(ks[0], (BATCH, SEQ, D_MODEL), 1.0)
    c = nrm(ks[1], (BATCH, D_MODEL), 1.0)
    positions = (jnp.arange(SEQ, dtype=jnp.int32)[None, :]
                 + jax.random.randint(ks[2], (BATCH, 1), 0, 1024, dtype=jnp.int32))
    w_mod = nrm(ks[3], (DEPTH, D_MODEL, 3 * D_MODEL), 0.1 * D_MODEL ** -0.5)
    b_mod = nrm(ks[4], (DEPTH, 3 * D_MODEL), 0.01)
    w_in = nrm(ks[5], (DEPTH, D_MODEL, IN_COLS), D_MODEL ** -0.5)
    w_out = nrm(ks[6], (DEPTH, MIX_W, D_MODEL), DEEPNORM_BETA * MIX_W ** -0.5)
    ln_g = 1.0 + nrm(ks[7], (DEPTH, D_MODEL), 0.02)
    ln_b = nrm(ks[8], (DEPTH, D_MODEL), 0.02)
    cmp_pe_k = nrm(ks[9], (DEPTH, CMP_LEN, HEAD_DIM), 0.02)
    cmp_pe_v = nrm(ks[10], (DEPTH, CMP_LEN, HEAD_DIM), 0.02)
    cmp_w1_k = nrm(ks[11], (DEPTH, CMP_LEN * HEAD_DIM, HEAD_DIM), (CMP_LEN * HEAD_DIM) ** -0.5)
    cmp_w2_k = nrm(ks[12], (DEPTH, HEAD_DIM, HEAD_DIM), HEAD_DIM ** -0.5)
    cmp_w1_v = nrm(ks[13], (DEPTH, CMP_LEN * HEAD_DIM, HEAD_DIM), (CMP_LEN * HEAD_DIM) ** -0.5)
    cmp_w2_v = nrm(ks[14], (DEPTH, HEAD_DIM, HEAD_DIM), HEAD_DIM ** -0.5)
    gdn_conv_w = nrm(ks[15], (DEPTH, GDN_CONV, 3 * GDN_W), GDN_CONV ** -0.5)
    gdn_a_log = jnp.log(jax.random.uniform(ks[16], (DEPTH, GDN_HEADS), f32, 1.0, 16.0))
    dt0 = jnp.exp(jax.random.uniform(ks[17], (DEPTH, GDN_HEADS), f32,
                                     math.log(1e-3), math.log(1e-1)))
    gdn_dt_bias = dt0 + jnp.log(-jnp.expm1(-dt0))
    gdn_norm_g = 1.0 + nrm(ks[18], (DEPTH, HEAD_DIM), 0.02)
    return {'x': x, 'c': c, 'positions': positions, 'w_mod': w_mod, 'b_mod': b_mod,
            'w_in': w_in, 'w_out': w_out, 'ln_g': ln_g, 'ln_b': ln_b,
            'cmp_pe_k': cmp_pe_k, 'cmp_pe_v': cmp_pe_v, 'cmp_w1_k': cmp_w1_k,
            'cmp_w2_k': cmp_w2_k, 'cmp_w1_v': cmp_w1_v, 'cmp_w2_v': cmp_w2_v,
            'gdn_conv_w': gdn_conv_w, 'gdn_a_log': gdn_a_log, 'gdn_dt_bias': gdn_dt_bias,
            'gdn_norm_g': gdn_norm_g}


def reference(x, c, positions, w_mod, b_mod, w_in, w_out, ln_g, ln_b,
              cmp_pe_k, cmp_pe_v, cmp_w1_k, cmp_w2_k, cmp_w1_v, cmp_w2_v,
              gdn_conv_w, gdn_a_log, gdn_dt_bias, gdn_norm_g):
    b, s, _ = x.shape
    dt = x.dtype
    f32 = jnp.float32
    c_act = jax.nn.silu(c)

    def heads(a, n):
        return a.reshape(b, s, n, HEAD_DIM)

    for l in range(DEPTH):
        mod = c_act @ w_mod[l] + b_mod[l]
        shift, scale, gate = jnp.split(mod, 3, axis=-1)
        h = x * (1.0 + scale[:, None, :]) + shift[:, None, :]
        p = h @ w_in[l]
        (nq, nkc, nvc, nks, nvs, nkw, nvw, ng, nz,
         gqkv, gb, ga, gz, sq, sk, sv, sz) = _split(p, IN_SPLITS)

        q_n = _rope(heads(nq, NSA_HEADS), positions)
        gates_n = jax.nn.sigmoid(ng.astype(f32)).reshape(b, s, NSA_HEADS, 3).astype(dt)
        o_nsa = _nsa(q_n, gates_n, heads(nkc, NSA_KV_HEADS), heads(nvc, NSA_KV_HEADS),
                     _rope(heads(nks, NSA_KV_HEADS), positions), heads(nvs, NSA_KV_HEADS),
                     _rope(heads(nkw, NSA_KV_HEADS), positions), heads(nvw, NSA_KV_HEADS),
                     positions, cmp_pe_k[l], cmp_pe_v[l], cmp_w1_k[l], cmp_w2_k[l],
                     cmp_w1_v[l], cmp_w2_v[l])
        y_nsa = o_nsa * jax.nn.silu(nz)

        qkv = jax.nn.silu(_causal_conv(gqkv, gdn_conv_w[l])).astype(f32)
        gq, gk, gv = jnp.split(qkv, 3, axis=-1)
        gq = heads(gq, GDN_HEADS)
        gk = heads(gk, GDN_HEADS)
        gv = heads(gv, GDN_HEADS)
        gq = gq * lax.rsqrt(jnp.sum(gq * gq, axis=-1, keepdims=True) + RMS_EPS)
        gk = gk * lax.rsqrt(jnp.sum(gk * gk, axis=-1, keepdims=True) + RMS_EPS)
        beta = jax.nn.sigmoid(gb.astype(f32))
        g = -jnp.exp(gdn_a_log[l].astype(f32)) * jax.nn.softplus(ga.astype(f32) + gdn_dt_bias[l])
        o_g = _gated_delta(gq, gk, gv, beta, g)
        o_g = o_g * lax.rsqrt(jnp.mean(o_g * o_g, axis=-1, keepdims=True) + RMS_EPS) * gdn_norm_g[l]
        y_gdn = o_g.astype(dt).reshape(b, s, GDN_W) * jax.nn.silu(gz)

        o_sb = _stick_breaking(heads(sq, SB_HEADS), heads(sk, SB_HEADS), heads(sv, SB_HEADS))
        y_sb = o_sb.reshape(b, s, SB_W) * jax.nn.silu(sz)

        y = jnp.concatenate([y_nsa, y_gdn, y_sb], axis=-1) @ w_out[l]
        x = _layernorm(DEEPNORM_ALPHA * x + (1.0 + gate[:, None, :]) * y, ln_g[l], ln_b[l])
    return x
```

```python
import functools
import math

import jax
import jax.numpy as jnp
import numpy as np
from jax import lax
from jax.experimental import pallas as pl
from jax.experimental.pallas import tpu as pltpu

F32 = jnp.float32
BF16 = jnp.bfloat16

LANE = 128
HEAD_DIM = 128
ROPE_DIM = HEAD_DIM // 4
ROPE_HALF = ROPE_DIM // 2
ROPE_THETA = 500000.0
Q_BLOCK = 128

NSA_HEADS = 6
NSA_KV_HEADS = 2
NSA_GROUP = NSA_HEADS // NSA_KV_HEADS
CMP_LEN = 32
CMP_STRIDE = 16
SLC_LEN = 64
SLC_TOPK = 16
WINDOW = 512
FORCE_BONUS = 1.0e4
GDN_HEADS = 6
GDN_CONV = 4
SB_HEADS = 4

NSA_W = NSA_HEADS * HEAD_DIM
KV_W = NSA_KV_HEADS * HEAD_DIM
GDN_W = GDN_HEADS * HEAD_DIM
SB_W = SB_HEADS * HEAD_DIM
MIX_W = NSA_W + GDN_W + SB_W

LN_EPS = 1e-5
RMS_EPS = 1e-6
NEG_INF = -1e30
ATTN_SCALE = HEAD_DIM ** -0.5

COL_NQ = 0
COL_NKC = COL_NQ + NSA_W
COL_NVC = COL_NKC + KV_W
COL_NKS = COL_NVC + KV_W
COL_NVS = COL_NKS + KV_W
COL_NKW = COL_NVS + KV_W
COL_NVW = COL_NKW + KV_W
COL_NZ = COL_NVW + KV_W
COL_GQ = COL_NZ + NSA_W
COL_GK = COL_GQ + GDN_W
COL_GV = COL_GK + GDN_W
COL_GZ = COL_GV + GDN_W
COL_SQ = COL_GZ + GDN_W
COL_SK = COL_SQ + SB_W
COL_SV = COL_SK + SB_W
COL_SZ = COL_SV + SB_W
COL_NG = COL_SZ + SB_W
COL_GBA = COL_NG + LANE
P_COLS = COL_GBA + LANE
GA_LANE = GDN_HEADS

GDN_CHUNK = 128
PROJ_TN = 768
VMEM_LIMIT = 56 * 1024 * 1024


def _cparams(*sem):
    return pltpu.CompilerParams(dimension_semantics=sem, vmem_limit_bytes=VMEM_LIMIT)


def _dot(a, b):
    return jnp.dot(a, b, preferred_element_type=F32)


def _dot_nt(a, b):
    return lax.dot_general(a, b, (((1,), (1,)), ((), ())), preferred_element_type=F32)


def _split3(x):
    hi = x.astype(BF16)
    r1 = x - hi.astype(F32)
    mid = r1.astype(BF16)
    lo = (r1 - mid.astype(F32)).astype(BF16)
    return hi, mid, lo


def _dot_exact_rhs(x, m_bf16):
    hi, mid, lo = _split3(x)
    return _dot(hi, m_bf16) + _dot(mid, m_bf16) + _dot(lo, m_bf16)


def _dot_exact_lhs(m_bf16, x):
    hi, mid, lo = _split3(x)
    return _dot(m_bf16, hi) + _dot(m_bf16, mid) + _dot(m_bf16, lo)


def _dot_hi(a, b):
    a_hi = a.astype(BF16)
    a_lo = (a - a_hi.astype(F32)).astype(BF16)
    b_hi = b.astype(BF16)
    b_lo = (b - b_hi.astype(F32)).astype(BF16)
    return _dot(a_hi, b_hi) + _dot(a_hi, b_lo) + _dot(a_lo, b_hi)


def _silu(x):
    return x * jax.nn.sigmoid(x)


def _rope_tile(x, cos_t, sin_a, sin_b):
    return (x * cos_t + pltpu.roll(x, LANE - ROPE_HALF, 1) * sin_a
            + pltpu.roll(x, ROPE_HALF, 1) * sin_b)


def _mod_kernel(c_ref, w_ref, b_ref, o_ref):
    c_act = _silu(c_ref[...])
    o_ref[...] = jnp.sum(c_act * w_ref[...], axis=0, keepdims=True) + b_ref[...]


def _modulation(c, w_mod, b_mod):
    depth, d, n = w_mod.shape
    tn = 512
    return pl.pallas_call(
        _mod_kernel,
        grid=(depth, n // tn),
        in_specs=[pl.BlockSpec((d, 1), lambda l, j: (0, 0)),
                  pl.BlockSpec((None, d, tn), lambda l, j: (l, 0, j)),
                  pl.BlockSpec((None, 1, tn), lambda l, j: (l, 0, j))],
        out_specs=pl.BlockSpec((None, 1, tn), lambda l, j: (l, 0, j)),
        out_shape=jax.ShapeDtypeStruct((depth, 1, n), F32),
        compiler_params=_cparams("parallel", "parallel"),
        name="adaln_mod",
    )(c.reshape(d, 1), w_mod, b_mod.reshape(depth, 1, n))


ROPE_TILES = frozenset(range(COL_NQ // LANE, COL_NKC // LANE)) | frozenset(
    range(COL_NKS // LANE, COL_NVS // LANE)) | frozenset(range(COL_NKW // LANE, COL_NVW // LANE))


def _inproj_kernel(x_ref, shift_ref, scale_ref, w_ref, cos_ref, sa_ref, sb_ref, o_ref, h_ref):
    j = pl.program_id(1)

    @pl.when(j == 0)
    def _():
        h_ref[...] = (x_ref[...] * (1.0 + scale_ref[...]) + shift_ref[...]).astype(BF16)

    acc = _dot(h_ref[...], w_ref[...])
    tiles_per_step = PROJ_TN // LANE
    rope_steps = sorted({t // tiles_per_step for t in ROPE_TILES})
    plain = j >= 0
    for step in rope_steps:
        plain = plain & (j != step)

        @pl.when(j == step)
        def _(step=step):
            cos_t, sin_a, sin_b = cos_ref[...], sa_ref[...], sb_ref[...]
            for t in range(tiles_per_step):
                tile = acc[:, t * LANE:(t + 1) * LANE]
                if step * tiles_per_step + t in ROPE_TILES:
                    tile = _rope_tile(tile, cos_t, sin_a, sin_b)
                o_ref[:, t * LANE:(t + 1) * LANE] = tile

    @pl.when(plain)
    def _():
        o_ref[...] = acc


def _in_projection(x, shift, scale, w_in_b, cos_t, sin_a, sin_b):
    s, d = x.shape
    n = w_in_b.shape[1]
    tm = min(1024, s)
    return pl.pallas_call(
        _inproj_kernel,
        grid=(s // tm, n // PROJ_TN),
        in_specs=[pl.BlockSpec((tm, d), lambda i, j: (i, 0)),
                  pl.BlockSpec((1, d), lambda i, j: (0, 0)),
                  pl.BlockSpec((1, d), lambda i, j: (0, 0)),
                  pl.BlockSpec((d, PROJ_TN), lambda i, j: (0, j)),
                  pl.BlockSpec((tm, LANE), lambda i, j: (i, 0)),
                  pl.BlockSpec((tm, LANE), lambda i, j: (i, 0)),
                  pl.BlockSpec((tm, LANE), lambda i, j: (i, 0))],
        out_specs=pl.BlockSpec((tm, PROJ_TN), lambda i, j: (i, j)),
        out_shape=jax.ShapeDtypeStruct((s, n), F32),
        scratch_shapes=[pltpu.VMEM((tm, d), BF16)],
        compiler_params=_cparams("parallel", "arbitrary"),
        name="in_projection",
    )(x, shift, scale, w_in_b, cos_t, sin_a, sin_b)


def _compress_kernel(k_ref, v_ref, pek_lo, pek_hi, pev_lo, pev_hi, wk_lo, wk_hi, wv_lo, wv_hi,
                     w2k_ref, w2v_ref, cos_ref, sa_ref, sb_ref, kc_ref, vc_ref,
                     ak_ref, bk_ref, av_ref, bv_ref):
    l = pl.program_id(0)

    @pl.when(l == 0)
    def _():
        for r in (ak_ref, bk_ref, av_ref, bv_ref):
            r[...] = jnp.zeros_like(r)

    kx = k_ref[...]
    vx = v_ref[...]
    ak_ref[...] += _dot((kx + pek_lo[...]).astype(BF16), wk_lo[...])
    bk_ref[...] += _dot((kx + pek_hi[...]).astype(BF16), wk_hi[...])
    av_ref[...] += _dot((vx + pev_lo[...]).astype(BF16), wv_lo[...])
    bv_ref[...] += _dot((vx + pev_hi[...]).astype(BF16), wv_hi[...])

    @pl.when(l == CMP_STRIDE - 1)
    def _():
        rows = ak_ref.shape[0]
        hk = ak_ref[...] + pltpu.roll(bk_ref[...], rows - 1, 0)
        hv = av_ref[...] + pltpu.roll(bv_ref[...], rows - 1, 0)
        kc = _dot(_silu(hk).astype(BF16), w2k_ref[...])
        vc = _dot(_silu(hv).astype(BF16), w2v_ref[...])
        cos_t, sin_a, sin_b = cos_ref[...], sa_ref[...], sb_ref[...]
        for g in range(NSA_KV_HEADS):
            sl = slice(g * LANE, (g + 1) * LANE)
            kc_ref[:, sl] = _rope_tile(kc[:, sl], cos_t, sin_a, sin_b).astype(BF16)
        vc_ref[...] = vc.astype(BF16)


def _block_diag2(w):
    z = jnp.zeros_like(w)
    return jnp.concatenate([jnp.concatenate([w, z], axis=-1),
                            jnp.concatenate([z, w], axis=-1)], axis=-2)


def _compress(p, pe_k, pe_v, w1_k, w2_k, w1_v, w2_v, cos_c, sa_c, sb_c):
    s = p.shape[0]
    rows = s // CMP_STRIDE
    pg = p.reshape(rows, CMP_STRIDE * P_COLS)
    blocks_per_tok = P_COLS // KV_W
    kcol, vcol = COL_NKC // KV_W, COL_NVC // KV_W

    def w1_parts(w1):
        w = w1.reshape(CMP_LEN, HEAD_DIM, HEAD_DIM).astype(BF16)
        return _block_diag2(w[:CMP_STRIDE]), _block_diag2(w[CMP_STRIDE:])

    def pe_parts(pe):
        t = jnp.concatenate([pe, pe], axis=-1).reshape(CMP_LEN, 1, KV_W)
        return t[:CMP_STRIDE], t[CMP_STRIDE:]

    wk_lo, wk_hi = w1_parts(w1_k)
    wv_lo, wv_hi = w1_parts(w1_v)
    pek_lo, pek_hi = pe_parts(pe_k)
    pev_lo, pev_hi = pe_parts(pe_v)
    w2k = _block_diag2(w2_k.astype(BF16))
    w2v = _block_diag2(w2_v.astype(BF16))

    tok_spec = lambda col: pl.BlockSpec((rows, KV_W), lambda l: (0, l * blocks_per_tok + col))
    pe_spec = pl.BlockSpec((None, 1, KV_W), lambda l: (l, 0, 0))
    w1_spec = pl.BlockSpec((None, KV_W, KV_W), lambda l: (l, 0, 0))
    full = lambda a: pl.BlockSpec(a.shape, lambda l: (0,) * a.ndim)
    return pl.pallas_call(
        _compress_kernel,
        grid=(CMP_STRIDE,),
        in_specs=[tok_spec(kcol), tok_spec(vcol), pe_spec, pe_spec, pe_spec, pe_spec,
                  w1_spec, w1_spec, w1_spec, w1_spec, full(w2k), full(w2v),
                  full(cos_c), full(sa_c), full(sb_c)],
        out_specs=[pl.BlockSpec((rows, KV_W), lambda l: (0, 0)),
                   pl.BlockSpec((rows, KV_W), lambda l: (0, 0))],
        out_shape=[jax.ShapeDtypeStruct((rows, KV_W), BF16),
                   jax.ShapeDtypeStruct((rows, KV_W), BF16)],
        scratch_shapes=[pltpu.VMEM((rows, KV_W), F32)] * 4,
        compiler_params=_cparams("arbitrary"),
        name="nsa_compress",
    )(pg, pg, pek_lo, pek_hi, pev_lo, pev_hi, wk_lo, wk_hi, wv_lo, wv_hi, w2k, w2v,
      cos_c, sa_c, sb_c)


def _softmax_rows(s, mask):
    s = jnp.where(mask, s, NEG_INF)
    m = jnp.max(s, axis=-1, keepdims=True)
    e = jnp.where(mask, jnp.exp(s - m), 0.0)
    return e / jnp.maximum(jnp.sum(e, axis=-1, keepdims=True), 1e-30)


def _select_top_blocks(imp):
    imp_t = imp.T
    nb = imp_t.shape[0]
    blk = lax.broadcasted_iota(jnp.int32, imp_t.shape, 0).astype(F32)

    def pick_one(_, carry):
        vals, sel = carry
        best = jnp.max(vals, axis=0, keepdims=True)
        first = jnp.min(jnp.where(vals == best, blk, float(nb)), axis=0, keepdims=True)
        hit = blk == first
        return jnp.where(hit, -jnp.inf, vals), jnp.where(hit, 1.0, sel)

    _, sel_t = lax.fori_loop(0, min(SLC_TOPK, nb), pick_one, (imp_t, jnp.zeros_like(imp_t)))
    return sel_t.T


def _nsa_kernel(q_ref, g_ref, z_ref, kc_ref, vc_ref, ov_ref, ks_ref, vs_ref, kw_ref, vw_ref,
                o_ref):
    i = pl.program_id(0)
    tq = q_ref.shape[0]
    n_cmp = kc_ref.shape[0]
    n_slc = ov_ref.shape[1]
    seq = ks_ref.shape[0]
    rows = NSA_GROUP * tq
    blocks_per_tile = LANE // SLC_LEN

    t_col = i * tq + lax.broadcasted_iota(jnp.int32, (tq, 1), 0)
    t_rows = jnp.concatenate([t_col] * NSA_GROUP, axis=0)
    gates = jax.nn.sigmoid(g_ref[...])

    cmp_end = lax.broadcasted_iota(jnp.int32, (1, n_cmp), 1) * CMP_STRIDE + (CMP_LEN - 1)
    cmp_mask = cmp_end <= t_rows
    slc_id = lax.broadcasted_iota(jnp.int32, (1, n_slc), 1)
    cur = t_col // SLC_LEN
    future = slc_id * SLC_LEN > t_col
    forced = (slc_id == 0) | (slc_id == cur) | (slc_id == cur - 1)

    win_len = WINDOW + tq
    win_start = pl.multiple_of(jnp.clip(i * tq - WINDOW, 0, seq - win_len), LANE)
    win_pos = win_start + lax.broadcasted_iota(jnp.int32, (1, win_len), 1)
    dpos = t_rows - win_pos
    win_mask = (dpos >= 0) & (dpos < WINDOW)

    for g in range(NSA_KV_HEADS):
        kv = slice(g * LANE, (g + 1) * LANE)
        heads = [g * NSA_GROUP + r for r in range(NSA_GROUP)]
        q = jnp.concatenate([q_ref[:, h * LANE:(h + 1) * LANE] for h in heads],
                            axis=0).astype(BF16)

        pc = _softmax_rows(_dot_nt(q, kc_ref[:, kv]) * ATTN_SCALE, cmp_mask)
        pc_b = pc.astype(BF16)
        o_c = _dot(pc_b, vc_ref[:, kv])

        imp = _dot(pc_b[0:tq], ov_ref[...])
        for r in range(1, NSA_GROUP):
            imp = imp + _dot(pc_b[r * tq:(r + 1) * tq], ov_ref[...])
        imp = jnp.where(future, NEG_INF, jnp.where(forced, imp + FORCE_BONUS, imp))
        sel = jnp.where(future, 0.0, _select_top_blocks(imp)).astype(BF16)

        def slc_step(m, carry):
            m_run, l_run, acc = carry
            start = pl.multiple_of(m * LANE, LANE)
            k = ks_ref[pl.ds(start, LANE), kv]
            v = vs_ref[pl.ds(start, LANE), kv]
            s = _dot_nt(q, k) * ATTN_SCALE
            key_pos = start + lax.broadcasted_iota(jnp.int32, (1, LANE), 1)
            blk_of_key = (lax.broadcasted_iota(jnp.int32, (n_slc, LANE), 1) // SLC_LEN
                          + m * blocks_per_tile)
            expand = (lax.broadcasted_iota(jnp.int32, (n_slc, LANE), 0) == blk_of_key)
            picked = _dot(sel, expand.astype(BF16))
            mask = ((jnp.concatenate([picked] * NSA_GROUP, axis=0) > 0.5)
                    & (key_pos <= t_rows))
            s = jnp.where(mask, s, NEG_INF)
            m_new = jnp.maximum(m_run, jnp.max(s, axis=-1, keepdims=True))
            p = jnp.where(mask, jnp.exp(s - m_new), 0.0)
            alpha = jnp.exp(m_run - m_new)
            l_new = alpha * l_run + jnp.sum(p, axis=-1, keepdims=True)
            acc = alpha * acc + _dot(p.astype(BF16), v)
            return m_new, l_new, acc

        init = (jnp.full((rows, 1), NEG_INF, F32), jnp.zeros((rows, 1), F32),
                jnp.zeros((rows, HEAD_DIM), F32))
        _, l_s, acc_s = lax.fori_loop(0, i + 1, slc_step, init)
        o_s = acc_s / jnp.maximum(l_s, 1e-30)

        kw = kw_ref[pl.ds(win_start, win_len), kv]
        vw = vw_ref[pl.ds(win_start, win_len), kv]
        pw = _softmax_rows(_dot_nt(q, kw) * ATTN_SCALE, win_mask)
        o_w = _dot(pw.astype(BF16), vw)

        for r, h in enumerate(heads):
            rs = slice(r * tq, (r + 1) * tq)
            mix = (gates[:, 3 * h:3 * h + 1] * o_c[rs] + gates[:, 3 * h + 1:3 * h + 2] * o_s[rs]
                   + gates[:, 3 * h + 2:3 * h + 3] * o_w[rs])
            hs = slice(h * LANE, (h + 1) * LANE)
            o_ref[:, hs] = (mix * _silu(z_ref[:, hs])).astype(o_ref.dtype)


def _overlap_matrix(n_cmp_rows, n_cmp, n_slc):
    starts = np.arange(n_cmp_rows) * CMP_STRIDE
    ends = starts + CMP_LEN
    blk = np.arange(n_slc)
    ov = np.minimum(ends[:, None], (blk[None, :] + 1) * SLC_LEN) - np.maximum(
        starts[:, None], blk[None, :] * SLC_LEN)
    ov = np.maximum(ov, 0).astype(np.float32) / CMP_LEN
    ov[n_cmp:] = 0.0
    return ov


def _nsa_attention(p, pb, kc, vc):
    s = p.shape[0]
    tq = Q_BLOCK
    n_cmp_rows = kc.shape[0]
    n_cmp = (s - CMP_LEN) // CMP_STRIDE + 1
    n_slc = s // SLC_LEN
    ov = jnp.asarray(_overlap_matrix(n_cmp_rows, n_cmp, n_slc), BF16)
    full = lambda a: pl.BlockSpec(a.shape, lambda i: (0,) * a.ndim)
    kv_spec = lambda col: pl.BlockSpec((s, KV_W), lambda i: (0, col // KV_W))
    return pl.pallas_call(
        _nsa_kernel,
        grid=(s // tq,),
        in_specs=[pl.BlockSpec((tq, NSA_W), lambda i: (i, COL_NQ // NSA_W)),
                  pl.BlockSpec((tq, LANE), lambda i: (i, COL_NG // LANE)),
                  pl.BlockSpec((tq, NSA_W), lambda i: (i, COL_NZ // NSA_W)),
                  full(kc), full(vc), full(ov),
                  kv_spec(COL_NKS), kv_spec(COL_NVS), kv_spec(COL_NKW), kv_spec(COL_NVW)],
        out_specs=pl.BlockSpec((tq, NSA_W), lambda i: (i, 0)),
        out_shape=jax.ShapeDtypeStruct((s, NSA_W), BF16),
        compiler_params=_cparams("parallel"),
        name="nsa_attention",
    )(p, p, p, kc, vc, ov, pb, pb, pb, pb)


def _unit_lower_inverse(low):
    n = low.shape[0]
    eye = (lax.broadcasted_iota(jnp.int32, (n, n), 0)
           == lax.broadcasted_iota(jnp.int32, (n, n), 1)).astype(F32)
    inv = eye - low
    power = low
    for _ in range(int(math.log2(n)) - 1):
        power = _dot_hi(power, power)
        inv = inv + _dot_hi(inv, power)
    return inv


def _gdn_prep_kernel(q_ref, k_ref, v_ref, qp_ref, kp_ref, vp_ref, cw_ref, gba_ref, arow_ref,
                     brow_ref, u_ref, w_ref, qh_ref, ktt_ref, intra_ref, eg_ref, xs_ref):
    n = pl.program_id(0)
    c = q_ref.shape[0]
    halo = qp_ref.shape[0]
    keep_halo = (n > 0).astype(F32)

    def conv_silu(cur_ref, prev_ref, col0):
        xs_ref[0:halo, :] = prev_ref[...] * keep_halo
        xs_ref[halo:halo + c, :] = cur_ref[...]
        y = jnp.zeros((c, GDN_W), F32)
        for tap in range(GDN_CONV):
            off = halo - (GDN_CONV - 1) + tap
            y = y + xs_ref[off:off + c, :] * cw_ref[tap:tap + 1, col0:col0 + GDN_W]
        return _silu(y)

    q_all = conv_silu(q_ref, qp_ref, 0)
    k_all = conv_silu(k_ref, kp_ref, GDN_W)
    v_all = conv_silu(v_ref, vp_ref, 2 * GDN_W)

    row = lax.broadcasted_iota(jnp.int32, (c, c), 0)
    col = lax.broadcasted_iota(jnp.int32, (c, c), 1)
    incl = row >= col
    strict = row > col
    tri = incl.astype(BF16)

    gba = gba_ref[...]
    beta_all = jax.nn.sigmoid(gba)
    x = gba + brow_ref[...]
    softplus = jnp.maximum(x, 0.0) + jnp.log1p(jnp.exp(-jnp.abs(x)))
    g_all = arow_ref[...] * softplus
    gc_all = _dot_exact_lhs(tri, g_all)
    eg_ref[...] = jnp.exp(gc_all[c - 1:c, :])

    for h in range(GDN_HEADS):
        hs = slice(h * LANE, (h + 1) * LANE)
        q = q_all[:, hs]
        k = k_all[:, hs]
        v = v_all[:, hs]
        q = q * lax.rsqrt(jnp.sum(q * q, axis=-1, keepdims=True) + RMS_EPS)
        k = k * lax.rsqrt(jnp.sum(k * k, axis=-1, keepdims=True) + RMS_EPS)
        q = q * ATTN_SCALE
        beta = beta_all[:, h:h + 1]
        gc = gc_all[:, GA_LANE + h:GA_LANE + h + 1]
        gc_row = jnp.sum(jnp.where(row == col, gc, 0.0), axis=0, keepdims=True)
        g_last = gc[c - 1:c, :]
        decay = jnp.where(incl, jnp.exp(jnp.where(incl, gc - gc_row, 0.0)), 0.0)
        kb = k * beta
        k_b16 = k.astype(BF16)
        low = jnp.where(strict, _dot_nt(kb.astype(BF16), k_b16) * decay, 0.0)
        inv = _unit_lower_inverse(low)
        u_ref[:, hs] = _dot_hi(inv, v * beta)
        w_ref[:, hs] = _dot_hi(inv, kb * jnp.exp(gc)).astype(BF16)
        intra_ref[:, hs] = (_dot_nt(q.astype(BF16), k_b16) * decay).astype(BF16)
        qh_ref[:, hs] = (q * jnp.exp(gc)).astype(BF16)
        ktt_ref[:, hs] = (k * jnp.exp(g_last - gc)).T.astype(BF16)


def _gdn_prep(p, conv_w, a_row, b_row):
    s = p.shape[0]
    c = GDN_CHUNK
    halo = 8
    nc = s // c
    cur = lambda col: pl.BlockSpec((c, GDN_W), lambda n: (n, col // GDN_W))
    prev = lambda col: pl.BlockSpec(
        (halo, GDN_W), lambda n: (jnp.maximum(n * (c // halo) - 1, 0), col // GDN_W))
    full = lambda a: pl.BlockSpec(a.shape, lambda n: (0,) * a.ndim)
    tok_out = pl.BlockSpec((c, GDN_W), lambda n: (n, 0))
    return pl.pallas_call(
        _gdn_prep_kernel,
        grid=(nc,),
        in_specs=[cur(COL_GQ), cur(COL_GK), cur(COL_GV), prev(COL_GQ), prev(COL_GK), prev(COL_GV),
                  full(conv_w), pl.BlockSpec((c, LANE), lambda n: (n, COL_GBA // LANE)),
                  full(a_row), full(b_row)],
        out_specs=[tok_out, tok_out, tok_out, tok_out, tok_out,
                   pl.BlockSpec((None, 1, LANE), lambda n: (n, 0, 0))],
        out_shape=[jax.ShapeDtypeStruct((s, GDN_W), F32),
                   jax.ShapeDtypeStruct((s, GDN_W), BF16),
                   jax.ShapeDtypeStruct((s, GDN_W), BF16),
                   jax.ShapeDtypeStruct((s, GDN_W), BF16),
                   jax.ShapeDtypeStruct((s, GDN_W), BF16),
                   jax.ShapeDtypeStruct((nc, 1, LANE), F32)],
        scratch_shapes=[pltpu.VMEM((halo + c, GDN_W), F32)],
        compiler_params=_cparams("parallel"),
        name="gdn_prep",
    )(p, p, p, p, p, p, conv_w, p, a_row, b_row)


def _gdn_scan_kernel(u_ref, w_ref, qh_ref, ktt_ref, intra_ref, eg_ref, z_ref, ng_ref, o_ref,
                     state_ref):
    @pl.when(pl.program_id(0) == 0)
    def _():
        state_ref[...] = jnp.zeros_like(state_ref)

    eg = eg_ref[...]
    for h in range(GDN_HEADS):
        hs = slice(h * LANE, (h + 1) * LANE)
        state = state_ref[h]
        state_b = state.astype(BF16)
        v_new = u_ref[:, hs] - _dot(w_ref[:, hs], state_b)
        v_new_b = v_new.astype(BF16)
        o = _dot(qh_ref[:, hs], state_b) + _dot(intra_ref[:, hs], v_new_b)
        state_ref[h] = state * eg[:, GA_LANE + h:GA_LANE + h + 1] + _dot(ktt_ref[:, hs], v_new_b)
        o = o * lax.rsqrt(jnp.mean(o * o, axis=-1, keepdims=True) + RMS_EPS) * ng_ref[...]
        o_ref[:, hs] = (o * _silu(z_ref[:, hs])).astype(o_ref.dtype)


def _gdn_scan(p, u, w, qh, ktt, intra, eg, norm_g):
    s = p.shape[0]
    c = GDN_CHUNK
    tok = pl.BlockSpec((c, GDN_W), lambda n: (n, 0))
    return pl.pallas_call(
        _gdn_scan_kernel,
        grid=(s // c,),
        in_specs=[tok, tok, tok, tok, tok,
                  pl.BlockSpec((None, 1, LANE), lambda n: (n, 0, 0)),
                  pl.BlockSpec((c, GDN_W), lambda n: (n, COL_GZ // GDN_W)),
                  pl.BlockSpec((1, LANE), lambda n: (0, 0))],
        out_specs=tok,
        out_shape=jax.ShapeDtypeStruct((s, GDN_W), BF16),
        scratch_shapes=[pltpu.VMEM((GDN_HEADS, HEAD_DIM, HEAD_DIM), F32)],
        compiler_params=_cparams("arbitrary"),
        name="gdn_scan",
    )(u, w, qh, ktt, intra, eg, p, norm_g)


def _sb_kernel(q_ref, k_ref, v_ref, z_ref, o_ref):
    i = pl.program_id(1)
    tq = q_ref.shape[0]
    q = q_ref[...]
    t_col = i * tq + lax.broadcasted_iota(jnp.int32, (tq, 1), 0)
    later = (lax.broadcasted_iota(jnp.int32, (LANE, LANE), 0)
             > lax.broadcasted_iota(jnp.int32, (LANE, LANE), 1)).astype(BF16)

    def step(it, carry):
        tail, acc = carry
        m = i - it
        start = pl.multiple_of(m * LANE, LANE)
        k = k_ref[pl.ds(start, LANE), :]
        v = v_ref[pl.ds(start, LANE), :]
        z = _dot_nt(q, k) * ATTN_SCALE
        past = (start + lax.broadcasted_iota(jnp.int32, (1, LANE), 1)) < t_col
        soft = jnp.log1p(jnp.exp(-jnp.abs(z)))
        log_beta = jnp.minimum(z, 0.0) - soft
        log_1m = jnp.where(past, jnp.minimum(-z, 0.0) - soft, 0.0)
        after = _dot_exact_rhs(log_1m, later) + tail
        a = jnp.where(past, jnp.exp(log_beta + after), 0.0)
        acc = acc + _dot(a.astype(BF16), v)
        return tail + jnp.sum(log_1m, axis=-1, keepdims=True), acc

    _, acc = lax.fori_loop(0, i + 1, step,
                           (jnp.zeros((tq, 1), F32), jnp.zeros((tq, HEAD_DIM), F32)))
    o_ref[...] = (acc * _silu(z_ref[...])).astype(o_ref.dtype)


def _stick_breaking(p, pb):
    s = p.shape[0]
    tq = Q_BLOCK
    return pl.pallas_call(
        _sb_kernel,
        grid=(SB_HEADS, s // tq),
        in_specs=[pl.BlockSpec((tq, LANE), lambda h, i: (i, COL_SQ // LANE + h)),
                  pl.BlockSpec((s, LANE), lambda h, i: (0, COL_SK // LANE + h)),
                  pl.BlockSpec((s, LANE), lambda h, i: (0, COL_SV // LANE + h)),
                  pl.BlockSpec((tq, LANE), lambda h, i: (i, COL_SZ // LANE + h))],
        out_specs=pl.BlockSpec((tq, LANE), lambda h, i: (i, h)),
        out_shape=jax.ShapeDtypeStruct((s, SB_W), BF16),
        compiler_params=_cparams("parallel", "parallel"),
        name="stick_breaking",
    )(pb, pb, pb, p)


def _outproj_kernel(alpha, yn_ref, yg_ref, ys_ref, x_ref, w_ref, gate_ref, g_ref, b_ref, o_ref):
    y = (_dot(yn_ref[...], w_ref[0:NSA_W, :])
         + _dot(yg_ref[...], w_ref[NSA_W:NSA_W + GDN_W, :])
         + _dot(ys_ref[...], w_ref[NSA_W + GDN_W:MIX_W, :]))
    r = alpha * x_ref[...] + (1.0 + gate_ref[...]) * y
    mu = jnp.mean(r, axis=-1, keepdims=True)
    var = jnp.mean(jnp.square(r - mu), axis=-1, keepdims=True)
    o_ref[...] = (r - mu) * lax.rsqrt(var + LN_EPS) * g_ref[...] + b_ref[...]


def _out_projection(y_nsa, y_gdn, y_sb, x, w_out_b, gate, ln_g, ln_b, alpha):
    s, d = x.shape
    tm = min(256, s)
    row = lambda i: (i, 0)
    const = lambda i: (0, 0)
    return pl.pallas_call(
        functools.partial(_outproj_kernel, alpha),
        grid=(s // tm,),
        in_specs=[pl.BlockSpec((tm, NSA_W), row), pl.BlockSpec((tm, GDN_W), row),
                  pl.BlockSpec((tm, SB_W), row), pl.BlockSpec((tm, d), row),
                  pl.BlockSpec((MIX_W, d), const), pl.BlockSpec((1, d), const),
                  pl.BlockSpec((1, d), const), pl.BlockSpec((1, d), const)],
        out_specs=pl.BlockSpec((tm, d), row),
        out_shape=jax.ShapeDtypeStruct((s, d), F32),
        compiler_params=_cparams("parallel"),
        name="out_projection",
    )(y_nsa, y_gdn, y_sb, x, w_out_b, gate, ln_g, ln_b)


def _aligned_w_in(w):
    d = w.shape[0]
    o = 0

    def take(width):
        nonlocal o
        seg = w[:, o:o + width]
        o += width
        return seg

    nsa_qkv = take(NSA_W + 6 * KV_W)
    ng = take(3 * NSA_HEADS)
    nz = take(NSA_W)
    gqkv = take(3 * GDN_W)
    gb = take(GDN_HEADS)
    ga = take(GDN_HEADS)
    gz = take(GDN_W)
    sb = take(4 * SB_W)
    pad = lambda a: jnp.pad(a, ((0, 0), (0, LANE - a.shape[1])))
    out = jnp.concatenate([nsa_qkv, nz, gqkv, gz, sb, pad(ng),
                           pad(jnp.concatenate([gb, ga], axis=1))], axis=1)
    assert out.shape == (d, P_COLS)
    return out.astype(BF16)


def _rope_tables(pos):
    inv = ROPE_THETA ** (-jnp.arange(ROPE_HALF, dtype=F32) * 2.0 / ROPE_DIM)
    ang = pos.astype(F32)[:, None] * inv
    cos, sin = jnp.cos(ang), jnp.sin(ang)
    n = pos.shape[0]
    rest = HEAD_DIM - ROPE_DIM
    cos_t = jnp.concatenate([cos, cos, jnp.ones((n, rest), F32)], axis=1)
    sin_a = jnp.concatenate([-sin, jnp.zeros((n, HEAD_DIM - ROPE_HALF), F32)], axis=1)
    sin_b = jnp.concatenate([jnp.zeros((n, ROPE_HALF), F32), sin, jnp.zeros((n, rest), F32)], axis=1)
    return cos_t, sin_a, sin_b


def kernel(x, c, positions, w_mod, b_mod, w_in, w_out, ln_g, ln_b, cmp_pe_k, cmp_pe_v, cmp_w1_k, cmp_w2_k, cmp_w1_v, cmp_w2_v, gdn_conv_w, gdn_a_log, gdn_dt_bias, gdn_norm_g):
    b, s, d = x.shape
    assert b == 1 and s % (8 * Q_BLOCK) == 0 and s >= WINDOW + Q_BLOCK
    depth = w_mod.shape[0]
    alpha = (2 * depth) ** 0.25
    xs = x[0]
    pos = positions[0]

    cos_t, sin_a, sin_b = _rope_tables(pos)
    n_rows = s // CMP_STRIDE
    cmp_end = jnp.minimum(jnp.arange(n_rows) * CMP_STRIDE + CMP_LEN - 1, s - 1)
    cos_c, sa_c, sb_c = _rope_tables(pos[cmp_end])

    mod = _modulation(c, w_mod, b_mod)
    lane_pad = lambda v: jnp.pad(v, (GA_LANE, LANE - GA_LANE - v.shape[0])).reshape(1, LANE)

    for l in range(depth):
        shift, scale, gate = (mod[l, :, k * d:(k + 1) * d] for k in range(3))
        p = _in_projection(xs, shift, scale, _aligned_w_in(w_in[l]), cos_t, sin_a, sin_b)
        pb = p.astype(BF16)

        kc, vc = _compress(p, cmp_pe_k[l], cmp_pe_v[l], cmp_w1_k[l], cmp_w2_k[l],
                           cmp_w1_v[l], cmp_w2_v[l], cos_c, sa_c, sb_c)
        y_nsa = _nsa_attention(p, pb, kc, vc)

        a_row = lane_pad(-jnp.exp(gdn_a_log[l].astype(F32)))
        b_row = lane_pad(gdn_dt_bias[l].astype(F32))
        u, w, qh, ktt, intra, eg = _gdn_prep(p, gdn_conv_w[l], a_row, b_row)
        y_gdn = _gdn_scan(p, u, w, qh, ktt, intra, eg, gdn_norm_g[l].reshape(1, LANE))

        y_sb = _stick_breaking(p, pb)

        xs = _out_projection(y_nsa, y_gdn, y_sb, xs, w_out[l].astype(BF16), gate,
                             ln_g[l].reshape(1, d), ln_b[l].reshape(1, d), alpha)
    return xs[None]
```

```python
import functools
import math

import jax
import jax.numpy as jnp
import numpy as np
from jax import lax
from jax.experimental import pallas as pl
from jax.experimental.pallas import tpu as pltpu

F32 = jnp.float32
BF16 = jnp.bfloat16

LANE = 128
HEAD_DIM = 128
ROPE_DIM = HEAD_DIM // 4
ROPE_HALF = ROPE_DIM // 2
ROPE_THETA = 500000.0
Q_BLOCK = 128

NSA_HEADS = 6
NSA_KV_HEADS = 2
NSA_GROUP = NSA_HEADS // NSA_KV_HEADS
CMP_LEN = 32
CMP_STRIDE = 16
SLC_LEN = 64
SLC_TOPK = 16
WINDOW = 512
FORCE_BONUS = 1.0e4
GDN_HEADS = 6
GDN_CONV = 4
SB_HEADS = 4

NSA_W = NSA_HEADS * HEAD_DIM
KV_W = NSA_KV_HEADS * HEAD_DIM
GDN_W = GDN_HEADS * HEAD_DIM
SB_W = SB_HEADS * HEAD_DIM
MIX_W = NSA_W + GDN_W + SB_W

LN_EPS = 1e-5
RMS_EPS = 1e-6
NEG_INF = -1e30
ATTN_SCALE = HEAD_DIM ** -0.5

COL_NQ = 0
COL_NKC = COL_NQ + NSA_W
COL_NVC = COL_NKC + KV_W
COL_NKS = COL_NVC + KV_W
COL_NVS = COL_NKS + KV_W
COL_NKW = COL_NVS + KV_W
COL_NVW = COL_NKW + KV_W
COL_NZ = COL_NVW + KV_W
COL_GQ = COL_NZ + NSA_W
COL_GK = COL_GQ + GDN_W
COL_GV = COL_GK + GDN_W
COL_GZ = COL_GV + GDN_W
COL_SQ = COL_GZ + GDN_W
COL_SK = COL_SQ + SB_W
COL_SV = COL_SK + SB_W
COL_SZ = COL_SV + SB_W
COL_NG = COL_SZ + SB_W
COL_GBA = COL_NG + LANE
P_COLS = COL_GBA + LANE
GA_LANE = GDN_HEADS

GDN_CHUNK = 128
PROJ_TN = 768
SLC_CHUNK = 512
SB_TQ = 512
SB_UNDERFLOW = -104.0
VMEM_LIMIT = 56 * 1024 * 1024


def _cparams(*sem):
    return pltpu.CompilerParams(dimension_semantics=sem, vmem_limit_bytes=VMEM_LIMIT)


def _dot(a, b):
    return jnp.dot(a, b, preferred_element_type=F32)


def _dot_nt(a, b):
    return lax.dot_general(a, b, (((1,), (1,)), ((), ())), preferred_element_type=F32)


def _split3(x):
    hi = x.astype(BF16)
    r1 = x - hi.astype(F32)
    mid = r1.astype(BF16)
    lo = (r1 - mid.astype(F32)).astype(BF16)
    return hi, mid, lo


def _dot_split_rhs(x, m_bf16):
    hi = x.astype(BF16)
    lo = (x - hi.astype(F32)).astype(BF16)
    return _dot(hi, m_bf16) + _dot(lo, m_bf16)


def _dot_exact_lhs(m_bf16, x):
    hi, mid, lo = _split3(x)
    return _dot(m_bf16, hi) + _dot(m_bf16, mid) + _dot(m_bf16, lo)


def _dot_hi(a, b):
    a_hi = a.astype(BF16)
    a_lo = (a - a_hi.astype(F32)).astype(BF16)
    b_hi = b.astype(BF16)
    b_lo = (b - b_hi.astype(F32)).astype(BF16)
    return _dot(a_hi, b_hi) + _dot(a_hi, b_lo) + _dot(a_lo, b_hi)


def _silu(x):
    return x * jax.nn.sigmoid(x)


def _rope_tile(x, cos_t, sin_a, sin_b):
    return (x * cos_t + pltpu.roll(x, LANE - ROPE_HALF, 1) * sin_a
            + pltpu.roll(x, ROPE_HALF, 1) * sin_b)


def _mod_kernel(c_ref, w_ref, b_ref, o_ref):
    c_act = _silu(c_ref[...])
    o_ref[...] = jnp.sum(c_act * w_ref[...], axis=0, keepdims=True) + b_ref[...]


def _modulation(c, w_mod, b_mod):
    depth, d, n = w_mod.shape
    tn = 512
    return pl.pallas_call(
        _mod_kernel,
        grid=(depth, n // tn),
        in_specs=[pl.BlockSpec((d, 1), lambda l, j: (0, 0)),
                  pl.BlockSpec((None, d, tn), lambda l, j: (l, 0, j)),
                  pl.BlockSpec((None, 1, tn), lambda l, j: (l, 0, j))],
        out_specs=pl.BlockSpec((None, 1, tn), lambda l, j: (l, 0, j)),
        out_shape=jax.ShapeDtypeStruct((depth, 1, n), F32),
        compiler_params=_cparams("parallel", "parallel"),
        name="adaln_mod",
    )(c.reshape(d, 1), w_mod, b_mod.reshape(depth, 1, n))


ROPE_TILES = frozenset(range(COL_NQ // LANE, COL_NKC // LANE)) | frozenset(
    range(COL_NKS // LANE, COL_NVS // LANE)) | frozenset(range(COL_NKW // LANE, COL_NVW // LANE))


def _inproj_kernel(x_ref, shift_ref, scale_ref, w_ref, cos_ref, sa_ref, sb_ref, o32_ref, o16_ref,
                   h_ref):
    j = pl.program_id(1)

    def emit(cols, tile):
        o32_ref[:, cols] = tile
        o16_ref[:, cols] = tile.astype(BF16)


    @pl.when(j == 0)
    def _():
        h_ref[...] = (x_ref[...] * (1.0 + scale_ref[...]) + shift_ref[...]).astype(BF16)

    acc = _dot(h_ref[...], w_ref[...])
    tiles_per_step = PROJ_TN // LANE
    rope_steps = sorted({t // tiles_per_step for t in ROPE_TILES})
    plain = j >= 0
    for step in rope_steps:
        plain = plain & (j != step)

        @pl.when(j == step)
        def _(step=step):
            cos_t, sin_a, sin_b = cos_ref[...], sa_ref[...], sb_ref[...]
            for t in range(tiles_per_step):
                tile = acc[:, t * LANE:(t + 1) * LANE]
                if step * tiles_per_step + t in ROPE_TILES:
                    tile = _rope_tile(tile, cos_t, sin_a, sin_b)
                emit(slice(t * LANE, (t + 1) * LANE), tile)

    @pl.when(plain)
    def _():
        emit(slice(None), acc)


def _in_projection(x, shift, scale, w_in_b, layer, cos_t, sin_a, sin_b):
    s, d = x.shape
    n = w_in_b.shape[2]
    tm = min(1024, s)
    return pl.pallas_call(
        _inproj_kernel,
        grid=(s // tm, n // PROJ_TN),
        in_specs=[pl.BlockSpec((tm, d), lambda i, j: (i, 0)),
                  pl.BlockSpec((1, d), lambda i, j: (0, 0)),
                  pl.BlockSpec((1, d), lambda i, j: (0, 0)),
                  pl.BlockSpec((None, d, PROJ_TN), lambda i, j: (layer, 0, j)),
                  pl.BlockSpec((tm, LANE), lambda i, j: (i, 0)),
                  pl.BlockSpec((tm, LANE), lambda i, j: (i, 0)),
                  pl.BlockSpec((tm, LANE), lambda i, j: (i, 0))],
        out_specs=[pl.BlockSpec((tm, PROJ_TN), lambda i, j: (i, j)),
                   pl.BlockSpec((tm, PROJ_TN), lambda i, j: (i, j))],
        out_shape=[jax.ShapeDtypeStruct((s, n), F32), jax.ShapeDtypeStruct((s, n), BF16)],
        scratch_shapes=[pltpu.VMEM((tm, d), BF16)],
        compiler_params=_cparams("parallel", "arbitrary"),
        name="in_projection",
    )(x, shift, scale, w_in_b, cos_t, sin_a, sin_b)


def _compress_kernel(k_ref, v_ref, pek_lo, pek_hi, pev_lo, pev_hi, wk_lo, wk_hi, wv_lo, wv_hi,
                     w2k_ref, w2v_ref, cos_ref, sa_ref, sb_ref, kc_ref, vc_ref,
                     ak_ref, bk_ref, av_ref, bv_ref):
    l = pl.program_id(0)

    @pl.when(l == 0)
    def _():
        for r in (ak_ref, bk_ref, av_ref, bv_ref):
            r[...] = jnp.zeros_like(r)

    kx = k_ref[...]
    vx = v_ref[...]
    ak_ref[...] += _dot((kx + pek_lo[...]).astype(BF16), wk_lo[...])
    bk_ref[...] += _dot((kx + pek_hi[...]).astype(BF16), wk_hi[...])
    av_ref[...] += _dot((vx + pev_lo[...]).astype(BF16), wv_lo[...])
    bv_ref[...] += _dot((vx + pev_hi[...]).astype(BF16), wv_hi[...])

    @pl.when(l == CMP_STRIDE - 1)
    def _():
        rows = ak_ref.shape[0]
        hk = ak_ref[...] + pltpu.roll(bk_ref[...], rows - 1, 0)
        hv = av_ref[...] + pltpu.roll(bv_ref[...], rows - 1, 0)
        kc = _dot(_silu(hk).astype(BF16), w2k_ref[...])
        vc = _dot(_silu(hv).astype(BF16), w2v_ref[...])
        cos_t, sin_a, sin_b = cos_ref[...], sa_ref[...], sb_ref[...]
        for g in range(NSA_KV_HEADS):
            sl = slice(g * LANE, (g + 1) * LANE)
            kc_ref[:, sl] = _rope_tile(kc[:, sl], cos_t, sin_a, sin_b).astype(BF16)
        vc_ref[...] = vc.astype(BF16)


def _block_diag2(w):
    z = jnp.zeros_like(w)
    return jnp.concatenate([jnp.concatenate([w, z], axis=-1),
                            jnp.concatenate([z, w], axis=-1)], axis=-2)


def _compress(p, pe_k, pe_v, w1_k, w2_k, w1_v, w2_v, cos_c, sa_c, sb_c):
    s = p.shape[0]
    rows = s // CMP_STRIDE
    assert COL_NVC == COL_NKC + KV_W
    kv_cols = lax.slice(p, (0, COL_NKC), (s, COL_NKC + 2 * KV_W))
    pg = kv_cols.reshape(rows, CMP_STRIDE * 2 * KV_W)
    blocks_per_tok = 2
    kcol, vcol = 0, 1

    def w1_parts(w1):
        w = w1.reshape(CMP_LEN, HEAD_DIM, HEAD_DIM).astype(BF16)
        return _block_diag2(w[:CMP_STRIDE]), _block_diag2(w[CMP_STRIDE:])

    def pe_parts(pe):
        t = jnp.concatenate([pe, pe], axis=-1).reshape(CMP_LEN, 1, KV_W)
        return t[:CMP_STRIDE], t[CMP_STRIDE:]

    wk_lo, wk_hi = w1_parts(w1_k)
    wv_lo, wv_hi = w1_parts(w1_v)
    pek_lo, pek_hi = pe_parts(pe_k)
    pev_lo, pev_hi = pe_parts(pe_v)
    w2k = _block_diag2(w2_k.astype(BF16))
    w2v = _block_diag2(w2_v.astype(BF16))

    tok_spec = lambda col: pl.BlockSpec((rows, KV_W), lambda l: (0, l * blocks_per_tok + col))
    pe_spec = pl.BlockSpec((None, 1, KV_W), lambda l: (l, 0, 0))
    w1_spec = pl.BlockSpec((None, KV_W, KV_W), lambda l: (l, 0, 0))
    full = lambda a: pl.BlockSpec(a.shape, lambda l: (0,) * a.ndim)
    return pl.pallas_call(
        _compress_kernel,
        grid=(CMP_STRIDE,),
        in_specs=[tok_spec(kcol), tok_spec(vcol), pe_spec, pe_spec, pe_spec, pe_spec,
                  w1_spec, w1_spec, w1_spec, w1_spec, full(w2k), full(w2v),
                  full(cos_c), full(sa_c), full(sb_c)],
        out_specs=[pl.BlockSpec((rows, KV_W), lambda l: (0, 0)),
                   pl.BlockSpec((rows, KV_W), lambda l: (0, 0))],
        out_shape=[jax.ShapeDtypeStruct((rows, KV_W), BF16),
                   jax.ShapeDtypeStruct((rows, KV_W), BF16)],
        scratch_shapes=[pltpu.VMEM((rows, KV_W), F32)] * 4,
        compiler_params=_cparams("arbitrary"),
        name="nsa_compress",
    )(pg, pg, pek_lo, pek_hi, pev_lo, pev_hi, wk_lo, wk_hi, wv_lo, wv_hi, w2k, w2v,
      cos_c, sa_c, sb_c)


def _softmax_rows(s, mask):
    s = jnp.where(mask, s, NEG_INF)
    m = jnp.max(s, axis=-1, keepdims=True)
    e = jnp.where(mask, jnp.exp(s - m), 0.0)
    return e / jnp.maximum(jnp.sum(e, axis=-1, keepdims=True), 1e-30)


def _select_top_blocks(imp):
    imp_t = imp.T
    nb = imp_t.shape[0]
    blk = lax.broadcasted_iota(jnp.int32, imp_t.shape, 0).astype(F32)

    def pick_one(_, carry):
        vals, sel = carry
        best = jnp.max(vals, axis=0, keepdims=True)
        first = jnp.min(jnp.where(vals == best, blk, float(nb)), axis=0, keepdims=True)
        hit = blk == first
        return jnp.where(hit, -jnp.inf, vals), jnp.where(hit, 1.0, sel)

    _, sel_t = lax.fori_loop(0, min(SLC_TOPK, nb), pick_one, (imp_t, jnp.zeros_like(imp_t)))
    return sel_t.T


def _nsa_kernel(q_ref, g_ref, z_ref, kc_ref, vc_ref, ov_ref, ks_ref, vs_ref, kw_ref, vw_ref,
                o_ref):
    i = pl.program_id(0)
    tq = q_ref.shape[0]
    n_cmp = kc_ref.shape[0]
    n_slc = ov_ref.shape[1]
    seq = ks_ref.shape[0]
    rows = NSA_GROUP * tq

    t_col = i * tq + lax.broadcasted_iota(jnp.int32, (tq, 1), 0)
    t_rows = jnp.concatenate([t_col] * NSA_GROUP, axis=0)
    gates = jax.nn.sigmoid(g_ref[...])

    cmp_end = lax.broadcasted_iota(jnp.int32, (1, n_cmp), 1) * CMP_STRIDE + (CMP_LEN - 1)
    cmp_mask = cmp_end <= t_rows
    slc_id = lax.broadcasted_iota(jnp.int32, (1, n_slc), 1)
    cur = t_col // SLC_LEN
    future = slc_id * SLC_LEN > t_col
    forced = (slc_id == 0) | (slc_id == cur) | (slc_id == cur - 1)

    win_len = WINDOW + tq
    win_start = pl.multiple_of(jnp.clip(i * tq - WINDOW, 0, seq - win_len), LANE)
    win_pos = win_start + lax.broadcasted_iota(jnp.int32, (1, win_len), 1)
    dpos = t_rows - win_pos
    win_mask = (dpos >= 0) & (dpos < WINDOW)

    groups = range(NSA_KV_HEADS)
    kv_cols = [slice(g * LANE, (g + 1) * LANE) for g in groups]
    heads_of = [[g * NSA_GROUP + r for r in range(NSA_GROUP)] for g in groups]
    qs, o_cs, sels = [], [], []
    for g in groups:
        q = jnp.concatenate([q_ref[:, h * LANE:(h + 1) * LANE] for h in heads_of[g]],
                            axis=0)
        pc = _softmax_rows(_dot_nt(q, kc_ref[:, kv_cols[g]]) * ATTN_SCALE, cmp_mask)
        pc_b = pc.astype(BF16)
        o_cs.append(_dot(pc_b, vc_ref[:, kv_cols[g]]))
        imp = _dot(pc_b[0:tq], ov_ref[...])
        for r in range(1, NSA_GROUP):
            imp = imp + _dot(pc_b[r * tq:(r + 1) * tq], ov_ref[...])
        imp = jnp.where(future, NEG_INF, jnp.where(forced, imp + FORCE_BONUS, imp))
        sels.append(jnp.where(future, 0.0, _select_top_blocks(imp)).astype(BF16))
        qs.append(q)

    blk_row = lax.broadcasted_iota(jnp.int32, (n_slc, SLC_CHUNK), 0)
    blk_col = lax.broadcasted_iota(jnp.int32, (n_slc, SLC_CHUNK), 1) // SLC_LEN
    lane_pos = lax.broadcasted_iota(jnp.int32, (1, SLC_CHUNK), 1)

    def slc_step(c, carry):
        start = pl.multiple_of(c * SLC_CHUNK, SLC_CHUNK)
        causal = (start + lane_pos) <= t_rows
        expand = (blk_row == blk_col + c * (SLC_CHUNK // SLC_LEN)).astype(BF16)
        out = []
        for g in groups:
            m_run, l_run, acc = carry[g]
            k = ks_ref[pl.ds(start, SLC_CHUNK), kv_cols[g]]
            v = vs_ref[pl.ds(start, SLC_CHUNK), kv_cols[g]]
            s = _dot_nt(qs[g], k) * ATTN_SCALE
            picked = _dot(sels[g], expand)
            mask = (jnp.concatenate([picked] * NSA_GROUP, axis=0) > 0.5) & causal
            s = jnp.where(mask, s, NEG_INF)
            m_new = jnp.maximum(m_run, jnp.max(s, axis=-1, keepdims=True))
            p = jnp.exp(s - m_new)
            alpha = jnp.exp(m_run - m_new)
            l_new = alpha * l_run + jnp.sum(p, axis=-1, keepdims=True)
            out.append((m_new, l_new, alpha * acc + _dot(p.astype(BF16), v)))
        return tuple(out)

    init = tuple((jnp.full((rows, 1), NEG_INF, F32), jnp.zeros((rows, 1), F32),
                  jnp.zeros((rows, HEAD_DIM), F32)) for _ in groups)
    slc = lax.fori_loop(0, (i * tq) // SLC_CHUNK + 1, slc_step, init)

    for g in groups:
        _, l_s, acc_s = slc[g]
        o_s = acc_s / jnp.maximum(l_s, 1e-30)
        kw = kw_ref[pl.ds(win_start, win_len), kv_cols[g]]
        vw = vw_ref[pl.ds(win_start, win_len), kv_cols[g]]
        pw = _softmax_rows(_dot_nt(qs[g], kw) * ATTN_SCALE, win_mask)
        o_w = _dot(pw.astype(BF16), vw)
        o_c = o_cs[g]
        for r, h in enumerate(heads_of[g]):
            rs = slice(r * tq, (r + 1) * tq)
            mix = (gates[:, 3 * h:3 * h + 1] * o_c[rs] + gates[:, 3 * h + 1:3 * h + 2] * o_s[rs]
                   + gates[:, 3 * h + 2:3 * h + 3] * o_w[rs])
            hs = slice(h * LANE, (h + 1) * LANE)
            o_ref[:, hs] = (mix * _silu(z_ref[:, hs])).astype(o_ref.dtype)


def _overlap_matrix(n_cmp_rows, n_cmp, n_slc):
    starts = np.arange(n_cmp_rows) * CMP_STRIDE
    ends = starts + CMP_LEN
    blk = np.arange(n_slc)
    ov = np.minimum(ends[:, None], (blk[None, :] + 1) * SLC_LEN) - np.maximum(
        starts[:, None], blk[None, :] * SLC_LEN)
    ov = np.maximum(ov, 0).astype(np.float32) / CMP_LEN
    ov[n_cmp:] = 0.0
    return ov


def _nsa_attention(p, pb, kc, vc):
    s = p.shape[0]
    tq = Q_BLOCK
    n_cmp_rows = kc.shape[0]
    n_cmp = (s - CMP_LEN) // CMP_STRIDE + 1
    n_slc = s // SLC_LEN
    ov = jnp.asarray(_overlap_matrix(n_cmp_rows, n_cmp, n_slc), BF16)
    full = lambda a: pl.BlockSpec(a.shape, lambda i: (0,) * a.ndim)
    kv_spec = lambda col: pl.BlockSpec((s, KV_W), lambda i: (0, col // KV_W))
    return pl.pallas_call(
        _nsa_kernel,
        grid=(s // tq,),
        in_specs=[pl.BlockSpec((tq, NSA_W), lambda i: (i, COL_NQ // NSA_W)),
                  pl.BlockSpec((tq, LANE), lambda i: (i, COL_NG // LANE)),
                  pl.BlockSpec((tq, NSA_W), lambda i: (i, COL_NZ // NSA_W)),
                  full(kc), full(vc), full(ov),
                  kv_spec(COL_NKS), kv_spec(COL_NVS), kv_spec(COL_NKW), kv_spec(COL_NVW)],
        out_specs=pl.BlockSpec((tq, NSA_W), lambda i: (i, 0)),
        out_shape=jax.ShapeDtypeStruct((s, NSA_W), BF16),
        compiler_params=_cparams("parallel"),
        name="nsa_attention",
    )(pb, p, p, kc, vc, ov, pb, pb, pb, pb)


def _unit_lower_inverse(low):
    n = low.shape[0]
    eye = (lax.broadcasted_iota(jnp.int32, (n, n), 0)
           == lax.broadcasted_iota(jnp.int32, (n, n), 1)).astype(F32)
    inv = eye - low
    power = low
    for _ in range(int(math.log2(n)) - 1):
        power = _dot_hi(power, power)
        inv = inv + _dot_hi(inv, power)
    return inv


def _gdn_prep_kernel(q_ref, k_ref, v_ref, qp_ref, kp_ref, vp_ref, cw_ref, gba_ref, arow_ref,
                     brow_ref, u_ref, w_ref, qh_ref, ktt_ref, intra_ref, eg_ref, xs_ref):
    n = pl.program_id(0)
    c = q_ref.shape[0]
    halo = qp_ref.shape[0]
    keep_halo = (n > 0).astype(F32)

    def conv_silu(cur_ref, prev_ref, col0):
        xs_ref[0:halo, :] = prev_ref[...] * keep_halo
        xs_ref[halo:halo + c, :] = cur_ref[...]
        y = jnp.zeros((c, GDN_W), F32)
        for tap in range(GDN_CONV):
            off = halo - (GDN_CONV - 1) + tap
            y = y + xs_ref[off:off + c, :] * cw_ref[tap:tap + 1, col0:col0 + GDN_W]
        return _silu(y)

    q_all = conv_silu(q_ref, qp_ref, 0)
    k_all = conv_silu(k_ref, kp_ref, GDN_W)
    v_all = conv_silu(v_ref, vp_ref, 2 * GDN_W)

    row = lax.broadcasted_iota(jnp.int32, (c, c), 0)
    col = lax.broadcasted_iota(jnp.int32, (c, c), 1)
    incl = row >= col
    strict = row > col
    tri = incl.astype(BF16)

    gba = gba_ref[...]
    beta_all = jax.nn.sigmoid(gba)
    x = gba + brow_ref[...]
    softplus = jnp.maximum(x, 0.0) + jnp.log1p(jnp.exp(-jnp.abs(x)))
    g_all = arow_ref[...] * softplus
    gc_all = _dot_exact_lhs(tri, g_all)
    eg_ref[...] = jnp.exp(gc_all[c - 1:c, :])

    for h in range(GDN_HEADS):
        hs = slice(h * LANE, (h + 1) * LANE)
        q = q_all[:, hs]
        k = k_all[:, hs]
        v = v_all[:, hs]
        q = q * lax.rsqrt(jnp.sum(q * q, axis=-1, keepdims=True) + RMS_EPS)
        k = k * lax.rsqrt(jnp.sum(k * k, axis=-1, keepdims=True) + RMS_EPS)
        q = q * ATTN_SCALE
        beta = beta_all[:, h:h + 1]
        gc = gc_all[:, GA_LANE + h:GA_LANE + h + 1]
        gc_row = jnp.sum(jnp.where(row == col, gc, 0.0), axis=0, keepdims=True)
        g_last = gc[c - 1:c, :]
        decay = jnp.where(incl, jnp.exp(jnp.where(incl, gc - gc_row, 0.0)), 0.0)
        kb = k * beta
        k_b16 = k.astype(BF16)
        low = jnp.where(strict, _dot_nt(kb.astype(BF16), k_b16) * decay, 0.0)
        inv = _unit_lower_inverse(low)
        u_ref[:, hs] = _dot_hi(inv, v * beta)
        w_ref[:, hs] = _dot_hi(inv, kb * jnp.exp(gc)).astype(BF16)
        intra_ref[:, hs] = (_dot_nt(q.astype(BF16), k_b16) * decay).astype(BF16)
        qh_ref[:, hs] = (q * jnp.exp(gc)).astype(BF16)
        ktt_ref[:, hs] = (k * jnp.exp(g_last - gc)).T.astype(BF16)


def _gdn_prep(p, conv_w, a_row, b_row):
    s = p.shape[0]
    c = GDN_CHUNK
    halo = 8
    nc = s // c
    cur = lambda col: pl.BlockSpec((c, GDN_W), lambda n: (n, col // GDN_W))
    prev = lambda col: pl.BlockSpec(
        (halo, GDN_W), lambda n: (jnp.maximum(n * (c // halo) - 1, 0), col // GDN_W))
    full = lambda a: pl.BlockSpec(a.shape, lambda n: (0,) * a.ndim)
    tok_out = pl.BlockSpec((c, GDN_W), lambda n: (n, 0))
    return pl.pallas_call(
        _gdn_prep_kernel,
        grid=(nc,),
        in_specs=[cur(COL_GQ), cur(COL_GK), cur(COL_GV), prev(COL_GQ), prev(COL_GK), prev(COL_GV),
                  full(conv_w), pl.BlockSpec((c, LANE), lambda n: (n, COL_GBA // LANE)),
                  full(a_row), full(b_row)],
        out_specs=[tok_out, tok_out, tok_out, tok_out, tok_out,
                   pl.BlockSpec((None, 1, LANE), lambda n: (n, 0, 0))],
        out_shape=[jax.ShapeDtypeStruct((s, GDN_W), F32),
                   jax.ShapeDtypeStruct((s, GDN_W), BF16),
                   jax.ShapeDtypeStruct((s, GDN_W), BF16),
                   jax.ShapeDtypeStruct((s, GDN_W), BF16),
                   jax.ShapeDtypeStruct((s, GDN_W), BF16),
                   jax.ShapeDtypeStruct((nc, 1, LANE), F32)],
        scratch_shapes=[pltpu.VMEM((halo + c, GDN_W), F32)],
        compiler_params=_cparams("parallel"),
        name="gdn_prep",
    )(p, p, p, p, p, p, conv_w, p, a_row, b_row)


def _gdn_scan_kernel(u_ref, w_ref, qh_ref, ktt_ref, intra_ref, eg_ref, z_ref, ng_ref, o_ref,
                     state_ref):
    @pl.when(pl.program_id(0) == 0)
    def _():
        state_ref[...] = jnp.zeros_like(state_ref)

    eg = eg_ref[...]
    for h in range(GDN_HEADS):
        hs = slice(h * LANE, (h + 1) * LANE)
        state = state_ref[h]
        state_b = state.astype(BF16)
        v_new = u_ref[:, hs] - _dot(w_ref[:, hs], state_b)
        v_new_b = v_new.astype(BF16)
        o = _dot(qh_ref[:, hs], state_b) + _dot(intra_ref[:, hs], v_new_b)
        state_ref[h] = state * eg[:, GA_LANE + h:GA_LANE + h + 1] + _dot(ktt_ref[:, hs], v_new_b)
        o = o * lax.rsqrt(jnp.mean(o * o, axis=-1, keepdims=True) + RMS_EPS) * ng_ref[...]
        o_ref[:, hs] = (o * _silu(z_ref[:, hs])).astype(o_ref.dtype)


def _gdn_scan(p, u, w, qh, ktt, intra, eg, norm_g):
    s = p.shape[0]
    c = GDN_CHUNK
    tok = pl.BlockSpec((c, GDN_W), lambda n: (n, 0))
    return pl.pallas_call(
        _gdn_scan_kernel,
        grid=(s // c,),
        in_specs=[tok, tok, tok, tok, tok,
                  pl.BlockSpec((None, 1, LANE), lambda n: (n, 0, 0)),
                  pl.BlockSpec((c, GDN_W), lambda n: (n, COL_GZ // GDN_W)),
                  pl.BlockSpec((1, LANE), lambda n: (0, 0))],
        out_specs=tok,
        out_shape=jax.ShapeDtypeStruct((s, GDN_W), BF16),
        scratch_shapes=[pltpu.VMEM((GDN_HEADS, HEAD_DIM, HEAD_DIM), F32)],
        compiler_params=_cparams("arbitrary"),
        name="gdn_scan",
    )(u, w, qh, ktt, intra, eg, p, norm_g)


def _sb_kernel(q_ref, k_ref, v_ref, z_ref, o_ref):
    i = pl.program_id(1)
    tq = q_ref.shape[0]
    q = q_ref[...]
    t_col = i * tq + lax.broadcasted_iota(jnp.int32, (tq, 1), 0)
    later = (lax.broadcasted_iota(jnp.int32, (LANE, LANE), 0)
             > lax.broadcasted_iota(jnp.int32, (LANE, LANE), 1)).astype(BF16)

    def tile_step(m, carry, on_diagonal):
        tail, acc = carry
        start = pl.multiple_of(m * LANE, LANE)
        k = k_ref[pl.ds(start, LANE), :]
        v = v_ref[pl.ds(start, LANE), :]
        z = _dot_nt(q, k) * ATTN_SCALE
        log_beta = jnp.minimum(z, 0.0) - jnp.log(1.0 + jnp.exp(-jnp.abs(z)))
        log_1m = log_beta - z
        if on_diagonal:
            past = (start + lax.broadcasted_iota(jnp.int32, (1, LANE), 1)) < t_col
            log_1m = jnp.where(past, log_1m, 0.0)
        a = jnp.exp(log_beta + _dot_split_rhs(log_1m, later) + tail)
        if on_diagonal:
            a = jnp.where(past, a, 0.0)
        acc = acc + _dot(a.astype(BF16), v)
        return tail + jnp.sum(log_1m, axis=-1, keepdims=True), acc

    diag_tiles = tq // LANE
    top = (i + 1) * diag_tiles - 1
    carry = (jnp.zeros((tq, 1), F32), jnp.zeros((tq, HEAD_DIM), F32))
    tail, acc = lax.fori_loop(0, diag_tiles, lambda it, c: tile_step(top - it, c, True), carry)

    def live(state):
        m, tail, _ = state
        return (m >= 0) & (jnp.max(tail) >= SB_UNDERFLOW)

    def older(state):
        m, tail, acc = state
        tail, acc = tile_step(m, (tail, acc), False)
        return m - 1, tail, acc

    _, _, acc = lax.while_loop(live, older, (i * diag_tiles - 1, tail, acc))
    o_ref[...] = (acc * _silu(z_ref[...])).astype(o_ref.dtype)


def _stick_breaking(p, pb):
    s = p.shape[0]
    tq = min(SB_TQ, s)
    return pl.pallas_call(
        _sb_kernel,
        grid=(SB_HEADS, s // tq),
        in_specs=[pl.BlockSpec((tq, LANE), lambda h, i: (i, COL_SQ // LANE + h)),
                  pl.BlockSpec((s, LANE), lambda h, i: (0, COL_SK // LANE + h)),
                  pl.BlockSpec((s, LANE), lambda h, i: (0, COL_SV // LANE + h)),
                  pl.BlockSpec((tq, LANE), lambda h, i: (i, COL_SZ // LANE + h))],
        out_specs=pl.BlockSpec((tq, LANE), lambda h, i: (i, h)),
        out_shape=jax.ShapeDtypeStruct((s, SB_W), BF16),
        compiler_params=_cparams("parallel", "parallel"),
        name="stick_breaking",
    )(pb, pb, pb, p)


def _outproj_kernel(alpha, yn_ref, yg_ref, ys_ref, x_ref, w_ref, gate_ref, g_ref, b_ref, o_ref):
    y = (_dot(yn_ref[...], w_ref[0:NSA_W, :])
         + _dot(yg_ref[...], w_ref[NSA_W:NSA_W + GDN_W, :])
         + _dot(ys_ref[...], w_ref[NSA_W + GDN_W:MIX_W, :]))
    r = alpha * x_ref[...] + (1.0 + gate_ref[...]) * y
    mu = jnp.mean(r, axis=-1, keepdims=True)
    var = jnp.mean(jnp.square(r - mu), axis=-1, keepdims=True)
    o_ref[...] = (r - mu) * lax.rsqrt(var + LN_EPS) * g_ref[...] + b_ref[...]


def _out_projection(y_nsa, y_gdn, y_sb, x, w_out_b, gate, ln_g, ln_b, alpha):
    s, d = x.shape
    tm = min(256, s)
    row = lambda i: (i, 0)
    const = lambda i: (0, 0)
    return pl.pallas_call(
        functools.partial(_outproj_kernel, alpha),
        grid=(s // tm,),
        in_specs=[pl.BlockSpec((tm, NSA_W), row), pl.BlockSpec((tm, GDN_W), row),
                  pl.BlockSpec((tm, SB_W), row), pl.BlockSpec((tm, d), row),
                  pl.BlockSpec((MIX_W, d), const), pl.BlockSpec((1, d), const),
                  pl.BlockSpec((1, d), const), pl.BlockSpec((1, d), const)],
        out_specs=pl.BlockSpec((tm, d), row),
        out_shape=jax.ShapeDtypeStruct((s, d), F32),
        compiler_params=_cparams("parallel"),
        name="out_projection",
    )(y_nsa, y_gdn, y_sb, x, w_out_b, gate, ln_g, ln_b)


def _segment_tables():
    nsa_qkv_w = NSA_W + 6 * KV_W
    segments = []
    o = 0
    for width, dst in ((nsa_qkv_w, COL_NQ), (3 * NSA_HEADS, COL_NG), (NSA_W, COL_NZ),
                       (3 * GDN_W, COL_GQ), (2 * GDN_HEADS, COL_GBA), (GDN_W, COL_GZ),
                       (4 * SB_W, COL_SQ)):
        segments.append((o, width, dst))
        o += width
    src = np.zeros(P_COLS // LANE, np.int32)
    shift = np.zeros_like(src)
    width_tab = np.zeros_like(src)
    for start, width, dst in segments:
        for off in range(0, width, LANE):
            j = (dst + off) // LANE
            src[j] = (start + off) // LANE
            shift[j] = (start + off) % LANE
            width_tab[j] = min(LANE, width - off)
    return src, shift, width_tab, o


def _wprep_kernel(total_cols, src_ref, shift_ref, width_ref, a_ref, b_ref, o_ref):
    j = pl.program_id(1)
    src, sh, wd = src_ref[j], shift_ref[j], width_ref[j]
    lane = lax.broadcasted_iota(jnp.int32, (1, LANE), 1)
    a = jnp.where(lane < total_cols - src * LANE, a_ref[...], 0.0).astype(BF16)
    b = jnp.where(lane < total_cols - (src + 1) * LANE, b_ref[...], 0.0).astype(BF16)
    r = lax.broadcasted_iota(jnp.int32, (LANE, LANE), 0)
    c = lax.broadcasted_iota(jnp.int32, (LANE, LANE), 1)
    from_a = ((r == c + sh) & (c < wd)).astype(BF16)
    from_b = ((r + LANE == c + sh) & (c < wd)).astype(BF16)
    o_ref[...] = (_dot(a, from_a) + _dot(b, from_b)).astype(BF16)


def _aligned_w_in(w_in):
    depth, d, n = w_in.shape
    src, shift, width_tab, total = _segment_tables()
    assert total == n
    last = (n - 1) // LANE
    grid_spec = pltpu.PrefetchScalarGridSpec(
        num_scalar_prefetch=3,
        grid=(depth, P_COLS // LANE),
        in_specs=[pl.BlockSpec((None, d, LANE), lambda l, j, s, *_: (l, 0, s[j])),
                  pl.BlockSpec((None, d, LANE),
                               lambda l, j, s, *_: (l, 0, jnp.minimum(s[j] + 1, last)))],
        out_specs=pl.BlockSpec((None, d, LANE), lambda l, j, *_: (l, 0, j)))
    return pl.pallas_call(
        functools.partial(_wprep_kernel, n),
        grid_spec=grid_spec,
        out_shape=jax.ShapeDtypeStruct((depth, d, P_COLS), BF16),
        compiler_params=_cparams("parallel", "parallel"),
        name="w_in_layout",
    )(jnp.asarray(src), jnp.asarray(shift), jnp.asarray(width_tab), w_in, w_in)


def _rope_tables(pos):
    inv = ROPE_THETA ** (-jnp.arange(ROPE_HALF, dtype=F32) * 2.0 / ROPE_DIM)
    ang = pos.astype(F32)[:, None] * inv
    cos, sin = jnp.cos(ang), jnp.sin(ang)
    n = pos.shape[0]
    rest = HEAD_DIM - ROPE_DIM
    cos_t = jnp.concatenate([cos, cos, jnp.ones((n, rest), F32)], axis=1)
    sin_a = jnp.concatenate([-sin, jnp.zeros((n, HEAD_DIM - ROPE_HALF), F32)], axis=1)
    sin_b = jnp.concatenate([jnp.zeros((n, ROPE_HALF), F32), sin, jnp.zeros((n, rest), F32)], axis=1)
    return cos_t, sin_a, sin_b


def kernel(x, c, positions, w_mod, b_mod, w_in, w_out, ln_g, ln_b, cmp_pe_k, cmp_pe_v, cmp_w1_k, cmp_w2_k, cmp_w1_v, cmp_w2_v, gdn_conv_w, gdn_a_log, gdn_dt_bias, gdn_norm_g):
    b, s, d = x.shape
    assert b == 1 and s % (8 * Q_BLOCK) == 0 and s >= WINDOW + Q_BLOCK
    depth = w_mod.shape[0]
    alpha = (2 * depth) ** 0.25
    xs = x[0]
    pos = positions[0]

    cos_t, sin_a, sin_b = _rope_tables(pos)
    n_rows = s // CMP_STRIDE
    cmp_end = jnp.minimum(jnp.arange(n_rows) * CMP_STRIDE + CMP_LEN - 1, s - 1)
    cos_c, sa_c, sb_c = _rope_tables(pos[cmp_end])

    mod = _modulation(c, w_mod, b_mod)
    lane_pad = lambda v: jnp.pad(v, (GA_LANE, LANE - GA_LANE - v.shape[0])).reshape(1, LANE)

    w_in_b = _aligned_w_in(w_in)

    for l in range(depth):
        shift, scale, gate = (mod[l, :, k * d:(k + 1) * d] for k in range(3))
        p, pb = _in_projection(xs, shift, scale, w_in_b, l, cos_t, sin_a, sin_b)

        kc, vc = _compress(p, cmp_pe_k[l], cmp_pe_v[l], cmp_w1_k[l], cmp_w2_k[l],
                           cmp_w1_v[l], cmp_w2_v[l], cos_c, sa_c, sb_c)
        y_nsa = _nsa_attention(p, pb, kc, vc)

        a_row = lane_pad(-jnp.exp(gdn_a_log[l].astype(F32)))
        b_row = lane_pad(gdn_dt_bias[l].astype(F32))
        u, w, qh, ktt, intra, eg = _gdn_prep(p, gdn_conv_w[l], a_row, b_row)
        y_gdn = _gdn_scan(p, u, w, qh, ktt, intra, eg, gdn_norm_g[l].reshape(1, LANE))

        y_sb = _stick_breaking(p, pb)

        xs = _out_projection(y_nsa, y_gdn, y_sb, xs, w_out[l].astype(BF16), gate,
                             ln_g[l].reshape(1, d), ln_b[l].reshape(1, d), alpha)
    return xs[None]
```

```python
import functools
import math

import jax
import jax.numpy as jnp
import numpy as np
from jax import lax
from jax.experimental import pallas as pl
from jax.experimental.pallas import tpu as pltpu

F32 = jnp.float32
BF16 = jnp.bfloat16

LANE = 128
HEAD_DIM = 128
ROPE_DIM = HEAD_DIM // 4
ROPE_HALF = ROPE_DIM // 2
ROPE_THETA = 500000.0
Q_BLOCK = 128

NSA_HEADS = 6
NSA_KV_HEADS = 2
NSA_GROUP = NSA_HEADS // NSA_KV_HEADS
CMP_LEN = 32
CMP_STRIDE = 16
SLC_LEN = 64
SLC_TOPK = 16
WINDOW = 512
FORCE_BONUS = 1.0e4
GDN_HEADS = 6
GDN_CONV = 4
SB_HEADS = 4

NSA_W = NSA_HEADS * HEAD_DIM
KV_W = NSA_KV_HEADS * HEAD_DIM
GDN_W = GDN_HEADS * HEAD_DIM
SB_W = SB_HEADS * HEAD_DIM
MIX_W = NSA_W + GDN_W + SB_W

LN_EPS = 1e-5
RMS_EPS = 1e-6
NEG_INF = -1e30
ATTN_SCALE = HEAD_DIM ** -0.5
EXP2_SCALE = ATTN_SCALE * math.log2(math.e)

COL_NQ = 0
COL_NKC = COL_NQ + NSA_W
COL_NVC = COL_NKC + KV_W
COL_NKS = COL_NVC + KV_W
COL_NVS = COL_NKS + KV_W
COL_NKW = COL_NVS + KV_W
COL_NVW = COL_NKW + KV_W
COL_NZ = COL_NVW + KV_W
COL_GQ = COL_NZ + NSA_W
COL_GK = COL_GQ + GDN_W
COL_GV = COL_GK + GDN_W
COL_GZ = COL_GV + GDN_W
COL_SQ = COL_GZ + GDN_W
COL_SK = COL_SQ + SB_W
COL_SV = COL_SK + SB_W
COL_SZ = COL_SV + SB_W
COL_NG = COL_SZ + SB_W
COL_GBA = COL_NG + LANE
P_COLS = COL_GBA + LANE
GA_LANE = GDN_HEADS

GDN_CHUNK = 128
PROJ_TN = 768
SLC_CHUNK = 512
INV_SPLIT_LEVELS = 2
SB_TQ = 512
SB_UNDERFLOW = -104.0
VMEM_LIMIT = 56 * 1024 * 1024


def _cparams(*sem):
    return pltpu.CompilerParams(dimension_semantics=sem, vmem_limit_bytes=VMEM_LIMIT)


def _dot(a, b):
    return jnp.dot(a, b, preferred_element_type=F32)


def _dot_nt(a, b):
    return lax.dot_general(a, b, (((1,), (1,)), ((), ())), preferred_element_type=F32)


def _split3(x):
    hi = x.astype(BF16)
    r1 = x - hi.astype(F32)
    mid = r1.astype(BF16)
    lo = (r1 - mid.astype(F32)).astype(BF16)
    return hi, mid, lo


def _dot_split_rhs(x, m_bf16):
    hi = x.astype(BF16)
    lo = (x - hi.astype(F32)).astype(BF16)
    return _dot(hi, m_bf16) + _dot(lo, m_bf16)


def _dot_exact_lhs(m_bf16, x):
    hi, mid, lo = _split3(x)
    return _dot(m_bf16, hi) + _dot(m_bf16, mid) + _dot(m_bf16, lo)


def _dot_hi(a, b):
    a_hi = a.astype(BF16)
    a_lo = (a - a_hi.astype(F32)).astype(BF16)
    b_hi = b.astype(BF16)
    b_lo = (b - b_hi.astype(F32)).astype(BF16)
    return _dot(a_hi, b_hi) + _dot(a_hi, b_lo) + _dot(a_lo, b_hi)


def _silu(x):
    return x * jax.nn.sigmoid(x)


def _rope_tile(x, cos_t, sin_a, sin_b):
    return (x * cos_t + pltpu.roll(x, LANE - ROPE_HALF, 1) * sin_a
            + pltpu.roll(x, ROPE_HALF, 1) * sin_b)


def _mod_kernel(c_ref, w_ref, b_ref, o_ref):
    c_act = _silu(c_ref[...])
    o_ref[...] = jnp.sum(c_act * w_ref[...], axis=0, keepdims=True) + b_ref[...]


def _modulation(c, w_mod, b_mod):
    depth, d, n = w_mod.shape
    tn = 512
    return pl.pallas_call(
        _mod_kernel,
        grid=(depth, n // tn),
        in_specs=[pl.BlockSpec((d, 1), lambda l, j: (0, 0)),
                  pl.BlockSpec((None, d, tn), lambda l, j: (l, 0, j)),
                  pl.BlockSpec((None, 1, tn), lambda l, j: (l, 0, j))],
        out_specs=pl.BlockSpec((None, 1, tn), lambda l, j: (l, 0, j)),
        out_shape=jax.ShapeDtypeStruct((depth, 1, n), F32),
        compiler_params=_cparams("parallel", "parallel"),
        name="adaln_mod",
    )(c.reshape(d, 1), w_mod, b_mod.reshape(depth, 1, n))


ROPE_TILES = frozenset(range(COL_NQ // LANE, COL_NKC // LANE)) | frozenset(
    range(COL_NKS // LANE, COL_NVS // LANE)) | frozenset(range(COL_NKW // LANE, COL_NVW // LANE))


def _inproj_kernel(x_ref, shift_ref, scale_ref, w_ref, cos_ref, sa_ref, sb_ref, o32_ref, o16_ref,
                   h_ref):
    j = pl.program_id(1)

    def emit(cols, tile):
        o32_ref[:, cols] = tile
        o16_ref[:, cols] = tile.astype(BF16)


    @pl.when(j == 0)
    def _():
        h_ref[...] = (x_ref[...] * (1.0 + scale_ref[...]) + shift_ref[...]).astype(BF16)

    acc = _dot_nt(h_ref[...], w_ref[...])
    tiles_per_step = PROJ_TN // LANE
    rope_steps = sorted({t // tiles_per_step for t in ROPE_TILES})
    plain = j >= 0
    for step in rope_steps:
        plain = plain & (j != step)

        @pl.when(j == step)
        def _(step=step):
            cos_t, sin_a, sin_b = cos_ref[...], sa_ref[...], sb_ref[...]
            for t in range(tiles_per_step):
                tile = acc[:, t * LANE:(t + 1) * LANE]
                if step * tiles_per_step + t in ROPE_TILES:
                    tile = _rope_tile(tile, cos_t, sin_a, sin_b)
                emit(slice(t * LANE, (t + 1) * LANE), tile)

    @pl.when(plain)
    def _():
        emit(slice(None), acc)


def _in_projection(x, shift, scale, w_in_b, layer, cos_t, sin_a, sin_b):
    s, d = x.shape
    n = w_in_b.shape[1]
    tm = min(1024, s)
    return pl.pallas_call(
        _inproj_kernel,
        grid=(s // tm, n // PROJ_TN),
        in_specs=[pl.BlockSpec((tm, d), lambda i, j: (i, 0)),
                  pl.BlockSpec((1, d), lambda i, j: (0, 0)),
                  pl.BlockSpec((1, d), lambda i, j: (0, 0)),
                  pl.BlockSpec((None, PROJ_TN, d), lambda i, j: (layer, j, 0)),
                  pl.BlockSpec((tm, LANE), lambda i, j: (i, 0)),
                  pl.BlockSpec((tm, LANE), lambda i, j: (i, 0)),
                  pl.BlockSpec((tm, LANE), lambda i, j: (i, 0))],
        out_specs=[pl.BlockSpec((tm, PROJ_TN), lambda i, j: (i, j)),
                   pl.BlockSpec((tm, PROJ_TN), lambda i, j: (i, j))],
        out_shape=[jax.ShapeDtypeStruct((s, n), F32), jax.ShapeDtypeStruct((s, n), BF16)],
        scratch_shapes=[pltpu.VMEM((tm, d), BF16)],
        compiler_params=_cparams("parallel", "arbitrary"),
        name="in_projection",
    )(x, shift, scale, w_in_b, cos_t, sin_a, sin_b)


def _compress_kernel(k_ref, v_ref, pek_lo, pek_hi, pev_lo, pev_hi, wk_lo, wk_hi, wv_lo, wv_hi,
                     w2k_ref, w2v_ref, cos_ref, sa_ref, sb_ref, kc_ref, vc_ref,
                     ak_ref, bk_ref, av_ref, bv_ref):
    l = pl.program_id(0)

    @pl.when(l == 0)
    def _():
        for r in (ak_ref, bk_ref, av_ref, bv_ref):
            r[...] = jnp.zeros_like(r)

    kx = k_ref[...]
    vx = v_ref[...]
    ak_ref[...] += _dot((kx + pek_lo[...]).astype(BF16), wk_lo[...])
    bk_ref[...] += _dot((kx + pek_hi[...]).astype(BF16), wk_hi[...])
    av_ref[...] += _dot((vx + pev_lo[...]).astype(BF16), wv_lo[...])
    bv_ref[...] += _dot((vx + pev_hi[...]).astype(BF16), wv_hi[...])

    @pl.when(l == CMP_STRIDE - 1)
    def _():
        rows = ak_ref.shape[0]
        hk = ak_ref[...] + pltpu.roll(bk_ref[...], rows - 1, 0)
        hv = av_ref[...] + pltpu.roll(bv_ref[...], rows - 1, 0)
        kc = _dot(_silu(hk).astype(BF16), w2k_ref[...])
        vc = _dot(_silu(hv).astype(BF16), w2v_ref[...])
        cos_t, sin_a, sin_b = cos_ref[...], sa_ref[...], sb_ref[...]
        for g in range(NSA_KV_HEADS):
            sl = slice(g * LANE, (g + 1) * LANE)
            kc_ref[:, sl] = _rope_tile(kc[:, sl], cos_t, sin_a, sin_b).astype(BF16)
        vc_ref[...] = vc.astype(BF16)


def _block_diag2(w):
    z = jnp.zeros_like(w)
    return jnp.concatenate([jnp.concatenate([w, z], axis=-1),
                            jnp.concatenate([z, w], axis=-1)], axis=-2)


def _compress(p, pe_k, pe_v, w1_k, w2_k, w1_v, w2_v, cos_c, sa_c, sb_c):
    s = p.shape[0]
    rows = s // CMP_STRIDE
    assert COL_NVC == COL_NKC + KV_W
    kv_cols = lax.slice(p, (0, COL_NKC), (s, COL_NKC + 2 * KV_W))
    pg = kv_cols.reshape(rows, CMP_STRIDE * 2 * KV_W)
    blocks_per_tok = 2
    kcol, vcol = 0, 1

    def w1_parts(w1):
        w = w1.reshape(CMP_LEN, HEAD_DIM, HEAD_DIM).astype(BF16)
        return _block_diag2(w[:CMP_STRIDE]), _block_diag2(w[CMP_STRIDE:])

    def pe_parts(pe):
        t = jnp.concatenate([pe, pe], axis=-1).reshape(CMP_LEN, 1, KV_W)
        return t[:CMP_STRIDE], t[CMP_STRIDE:]

    wk_lo, wk_hi = w1_parts(w1_k)
    wv_lo, wv_hi = w1_parts(w1_v)
    pek_lo, pek_hi = pe_parts(pe_k)
    pev_lo, pev_hi = pe_parts(pe_v)
    w2k = _block_diag2(w2_k.astype(BF16))
    w2v = _block_diag2(w2_v.astype(BF16))

    tok_spec = lambda col: pl.BlockSpec((rows, KV_W), lambda l: (0, l * blocks_per_tok + col))
    pe_spec = pl.BlockSpec((None, 1, KV_W), lambda l: (l, 0, 0))
    w1_spec = pl.BlockSpec((None, KV_W, KV_W), lambda l: (l, 0, 0))
    full = lambda a: pl.BlockSpec(a.shape, lambda l: (0,) * a.ndim)
    return pl.pallas_call(
        _compress_kernel,
        grid=(CMP_STRIDE,),
        in_specs=[tok_spec(kcol), tok_spec(vcol), pe_spec, pe_spec, pe_spec, pe_spec,
                  w1_spec, w1_spec, w1_spec, w1_spec, full(w2k), full(w2v),
                  full(cos_c), full(sa_c), full(sb_c)],
        out_specs=[pl.BlockSpec((rows, KV_W), lambda l: (0, 0)),
                   pl.BlockSpec((rows, KV_W), lambda l: (0, 0))],
        out_shape=[jax.ShapeDtypeStruct((rows, KV_W), BF16),
                   jax.ShapeDtypeStruct((rows, KV_W), BF16)],
        scratch_shapes=[pltpu.VMEM((rows, KV_W), F32)] * 4,
        compiler_params=_cparams("arbitrary"),
        name="nsa_compress",
    )(pg, pg, pek_lo, pek_hi, pev_lo, pev_hi, wk_lo, wk_hi, wv_lo, wv_hi, w2k, w2v,
      cos_c, sa_c, sb_c)


def _softmax_rows(s, mask):
    s = jnp.where(mask, s, NEG_INF)
    m = jnp.max(s, axis=-1, keepdims=True)
    e = jnp.where(mask, jnp.exp(s - m), 0.0)
    return e / jnp.maximum(jnp.sum(e, axis=-1, keepdims=True), 1e-30)


def _select_top_blocks(imp):
    imp_t = imp.T
    nb = imp_t.shape[0]
    blk = lax.broadcasted_iota(jnp.int32, imp_t.shape, 0).astype(F32)

    def pick_one(_, carry):
        vals, sel = carry
        best = jnp.max(vals, axis=0, keepdims=True)
        first = jnp.min(jnp.where(vals == best, blk, float(nb)), axis=0, keepdims=True)
        hit = blk == first
        return jnp.where(hit, -jnp.inf, vals), jnp.where(hit, 1.0, sel)

    _, sel_t = lax.fori_loop(0, min(SLC_TOPK, nb), pick_one, (imp_t, jnp.zeros_like(imp_t)))
    return sel_t.T


def _nsa_kernel(q_ref, g_ref, z_ref, kc_ref, vc_ref, ov_ref, ks_ref, vs_ref, kw_ref, vw_ref,
                o_ref):
    i = pl.program_id(0)
    tq = q_ref.shape[0]
    n_cmp = kc_ref.shape[0]
    n_slc = ov_ref.shape[1]
    seq = ks_ref.shape[0]
    rows = NSA_GROUP * tq

    t_col = i * tq + lax.broadcasted_iota(jnp.int32, (tq, 1), 0)
    t_rows = jnp.concatenate([t_col] * NSA_GROUP, axis=0)
    gates = jax.nn.sigmoid(g_ref[...])

    cmp_end = lax.broadcasted_iota(jnp.int32, (1, n_cmp), 1) * CMP_STRIDE + (CMP_LEN - 1)
    cmp_mask = cmp_end <= t_rows
    slc_id = lax.broadcasted_iota(jnp.int32, (1, n_slc), 1)
    cur = t_col // SLC_LEN
    future = slc_id * SLC_LEN > t_col
    forced = (slc_id == 0) | (slc_id == cur) | (slc_id == cur - 1)

    win_len = WINDOW + tq
    win_start = pl.multiple_of(jnp.clip(i * tq - WINDOW, 0, seq - win_len), LANE)
    win_pos = win_start + lax.broadcasted_iota(jnp.int32, (1, win_len), 1)
    dpos = t_rows - win_pos
    win_mask = (dpos >= 0) & (dpos < WINDOW)

    groups = range(NSA_KV_HEADS)
    kv_cols = [slice(g * LANE, (g + 1) * LANE) for g in groups]
    heads_of = [[g * NSA_GROUP + r for r in range(NSA_GROUP)] for g in groups]
    qs, o_cs, sels = [], [], []
    for g in groups:
        q = jnp.concatenate([q_ref[:, h * LANE:(h + 1) * LANE] for h in heads_of[g]],
                            axis=0)
        pc = _softmax_rows(_dot_nt(q, kc_ref[:, kv_cols[g]]) * ATTN_SCALE, cmp_mask)
        pc_b = pc.astype(BF16)
        o_cs.append(_dot(pc_b, vc_ref[:, kv_cols[g]]))
        imp = _dot(pc_b[0:tq], ov_ref[...])
        for r in range(1, NSA_GROUP):
            imp = imp + _dot(pc_b[r * tq:(r + 1) * tq], ov_ref[...])
        imp = jnp.where(future, NEG_INF, jnp.where(forced, imp + FORCE_BONUS, imp))
        sels.append(jnp.where(future, 0.0, _select_top_blocks(imp)).astype(BF16))
        qs.append(q)

    blk_row = lax.broadcasted_iota(jnp.int32, (n_slc, SLC_CHUNK), 0)
    blk_col = lax.broadcasted_iota(jnp.int32, (n_slc, SLC_CHUNK), 1) // SLC_LEN
    lane_pos = lax.broadcasted_iota(jnp.int32, (1, SLC_CHUNK), 1)

    def slc_step(c, carry):
        start = pl.multiple_of(c * SLC_CHUNK, SLC_CHUNK)
        causal = (start + lane_pos) <= t_col
        expand = (blk_row == blk_col + c * (SLC_CHUNK // SLC_LEN)).astype(BF16)
        out = []
        for g in groups:
            m_old, l_old, acc = carry[g]
            k = ks_ref[pl.ds(start, SLC_CHUNK), kv_cols[g]]
            v = vs_ref[pl.ds(start, SLC_CHUNK), kv_cols[g]]
            picked = _dot(sels[g], expand)
            bias = jnp.where((picked > 0.5) & causal, 0.0, NEG_INF)
            s = _dot_nt(qs[g], k) * EXP2_SCALE + jnp.concatenate([bias] * NSA_GROUP, axis=0)
            m_new = jnp.maximum(m_old, jnp.max(s, axis=-1, keepdims=True))
            p = jnp.exp2(s - m_new)
            alpha = jnp.exp2(m_old - m_new)
            out.append((m_new, alpha * l_old + jnp.sum(p, axis=-1, keepdims=True),
                        alpha * acc + _dot(p.astype(BF16), v)))
        return tuple(out)

    init = tuple((jnp.full((rows, 1), NEG_INF, F32), jnp.zeros((rows, 1), F32),
                  jnp.zeros((rows, HEAD_DIM), F32)) for _ in groups)
    slc = lax.fori_loop(0, (i * tq) // SLC_CHUNK + 1, slc_step, init)

    for g in groups:
        _, l_s, acc_s = slc[g]
        o_s = acc_s / jnp.maximum(l_s, 1e-30)
        kw = kw_ref[pl.ds(win_start, win_len), kv_cols[g]]
        vw = vw_ref[pl.ds(win_start, win_len), kv_cols[g]]
        pw = _softmax_rows(_dot_nt(qs[g], kw) * ATTN_SCALE, win_mask)
        o_w = _dot(pw.astype(BF16), vw)
        o_c = o_cs[g]
        for r, h in enumerate(heads_of[g]):
            rs = slice(r * tq, (r + 1) * tq)
            mix = (gates[:, 3 * h:3 * h + 1] * o_c[rs] + gates[:, 3 * h + 1:3 * h + 2] * o_s[rs]
                   + gates[:, 3 * h + 2:3 * h + 3] * o_w[rs])
            hs = slice(h * LANE, (h + 1) * LANE)
            o_ref[:, hs] = (mix * _silu(z_ref[:, hs])).astype(o_ref.dtype)


def _overlap_matrix(n_cmp_rows, n_cmp, n_slc):
    starts = np.arange(n_cmp_rows) * CMP_STRIDE
    ends = starts + CMP_LEN
    blk = np.arange(n_slc)
    ov = np.minimum(ends[:, None], (blk[None, :] + 1) * SLC_LEN) - np.maximum(
        starts[:, None], blk[None, :] * SLC_LEN)
    ov = np.maximum(ov, 0).astype(np.float32) / CMP_LEN
    ov[n_cmp:] = 0.0
    return ov


def _nsa_attention(p, pb, kc, vc):
    s = p.shape[0]
    tq = Q_BLOCK
    rows = NSA_GROUP * tq
    n_cmp_rows = kc.shape[0]
    n_cmp = (s - CMP_LEN) // CMP_STRIDE + 1
    n_slc = s // SLC_LEN
    ov = jnp.asarray(_overlap_matrix(n_cmp_rows, n_cmp, n_slc), BF16)
    full = lambda a: pl.BlockSpec(a.shape, lambda i: (0,) * a.ndim)
    kv_spec = lambda col: pl.BlockSpec((s, KV_W), lambda i: (0, col // KV_W))
    return pl.pallas_call(
        _nsa_kernel,
        grid=(s // tq,),
        in_specs=[pl.BlockSpec((tq, NSA_W), lambda i: (i, COL_NQ // NSA_W)),
                  pl.BlockSpec((tq, LANE), lambda i: (i, COL_NG // LANE)),
                  pl.BlockSpec((tq, NSA_W), lambda i: (i, COL_NZ // NSA_W)),
                  full(kc), full(vc), full(ov),
                  kv_spec(COL_NKS), kv_spec(COL_NVS), kv_spec(COL_NKW), kv_spec(COL_NVW)],
        out_specs=pl.BlockSpec((tq, NSA_W), lambda i: (i, 0)),
        out_shape=jax.ShapeDtypeStruct((s, NSA_W), BF16),
        compiler_params=_cparams("parallel"),
        name="nsa_attention",
    )(pb, p, p, kc, vc, ov, pb, pb, pb, pb)


def _unit_lower_inverses(lows):
    n = lows[0].shape[0]
    eye = (lax.broadcasted_iota(jnp.int32, (n, n), 0)
           == lax.broadcasted_iota(jnp.int32, (n, n), 1)).astype(F32)
    invs = [eye - low for low in lows]
    powers = list(lows)
    for level in range(int(math.log2(n)) - 1):
        mm = _dot_hi if level < INV_SPLIT_LEVELS else (
            lambda a, b: _dot(a.astype(BF16), b.astype(BF16)))
        powers = [mm(pw, pw) for pw in powers]
        invs = [inv + mm(inv, pw) for inv, pw in zip(invs, powers)]
    return invs


def _gdn_prep_kernel(q_ref, k_ref, v_ref, qp_ref, kp_ref, vp_ref, cw_ref, gba_ref, arow_ref,
                     brow_ref, u_ref, w_ref, qh_ref, ktt_ref, intra_ref, eg_ref, xs_ref):
    n = pl.program_id(0)
    c = q_ref.shape[0]
    halo = qp_ref.shape[0]
    keep_halo = (n > 0).astype(F32)

    def conv_silu(cur_ref, prev_ref, col0):
        xs_ref[0:halo, :] = prev_ref[...] * keep_halo
        xs_ref[halo:halo + c, :] = cur_ref[...]
        y = jnp.zeros((c, GDN_W), F32)
        for tap in range(GDN_CONV):
            off = halo - (GDN_CONV - 1) + tap
            y = y + xs_ref[off:off + c, :] * cw_ref[tap:tap + 1, col0:col0 + GDN_W]
        return _silu(y)

    q_all = conv_silu(q_ref, qp_ref, 0)
    k_all = conv_silu(k_ref, kp_ref, GDN_W)
    v_all = conv_silu(v_ref, vp_ref, 2 * GDN_W)

    row = lax.broadcasted_iota(jnp.int32, (c, c), 0)
    col = lax.broadcasted_iota(jnp.int32, (c, c), 1)
    incl = row >= col
    strict = row > col
    tri = incl.astype(BF16)

    gba = gba_ref[...]
    beta_all = jax.nn.sigmoid(gba)
    x = gba + brow_ref[...]
    softplus = jnp.maximum(x, 0.0) + jnp.log1p(jnp.exp(-jnp.abs(x)))
    g_all = arow_ref[...] * softplus
    gc_all = _dot_exact_lhs(tri, g_all)
    eg_ref[...] = jnp.exp(gc_all[c - 1:c, :])

    lows, rhs_u, rhs_w = [], [], []
    for h in range(GDN_HEADS):
        hs = slice(h * LANE, (h + 1) * LANE)
        q = q_all[:, hs]
        k = k_all[:, hs]
        q = q * lax.rsqrt(jnp.sum(q * q, axis=-1, keepdims=True) + RMS_EPS)
        k = k * lax.rsqrt(jnp.sum(k * k, axis=-1, keepdims=True) + RMS_EPS)
        q = q * ATTN_SCALE
        beta = beta_all[:, h:h + 1]
        gc = gc_all[:, GA_LANE + h:GA_LANE + h + 1]
        gc_row = jnp.sum(jnp.where(row == col, gc, 0.0), axis=0, keepdims=True)
        g_last = gc[c - 1:c, :]
        decay = jnp.where(incl, jnp.exp(jnp.where(incl, gc - gc_row, 0.0)), 0.0)
        kb = k * beta
        k_b16 = k.astype(BF16)
        lows.append(jnp.where(strict, _dot_nt(kb.astype(BF16), k_b16) * decay, 0.0))
        rhs_u.append(v_all[:, hs] * beta)
        rhs_w.append(kb * jnp.exp(gc))
        intra_ref[:, hs] = (_dot_nt(q.astype(BF16), k_b16) * decay).astype(BF16)
        qh_ref[:, hs] = (q * jnp.exp(gc)).astype(BF16)
        ktt_ref[:, hs] = (k * jnp.exp(g_last - gc)).T.astype(BF16)

    for h, inv in enumerate(_unit_lower_inverses(lows)):
        hs = slice(h * LANE, (h + 1) * LANE)
        u_ref[:, hs] = _dot_hi(inv, rhs_u[h])
        w_ref[:, hs] = _dot_hi(inv, rhs_w[h]).astype(BF16)


def _gdn_prep(p, conv_w, a_row, b_row):
    s = p.shape[0]
    c = GDN_CHUNK
    halo = 8
    nc = s // c
    cur = lambda col: pl.BlockSpec((c, GDN_W), lambda n: (n, col // GDN_W))
    prev = lambda col: pl.BlockSpec(
        (halo, GDN_W), lambda n: (jnp.maximum(n * (c // halo) - 1, 0), col // GDN_W))
    full = lambda a: pl.BlockSpec(a.shape, lambda n: (0,) * a.ndim)
    tok_out = pl.BlockSpec((c, GDN_W), lambda n: (n, 0))
    return pl.pallas_call(
        _gdn_prep_kernel,
        grid=(nc,),
        in_specs=[cur(COL_GQ), cur(COL_GK), cur(COL_GV), prev(COL_GQ), prev(COL_GK), prev(COL_GV),
                  full(conv_w), pl.BlockSpec((c, LANE), lambda n: (n, COL_GBA // LANE)),
                  full(a_row), full(b_row)],
        out_specs=[tok_out, tok_out, tok_out, tok_out, tok_out,
                   pl.BlockSpec((None, 1, LANE), lambda n: (n, 0, 0))],
        out_shape=[jax.ShapeDtypeStruct((s, GDN_W), F32),
                   jax.ShapeDtypeStruct((s, GDN_W), BF16),
                   jax.ShapeDtypeStruct((s, GDN_W), BF16),
                   jax.ShapeDtypeStruct((s, GDN_W), BF16),
                   jax.ShapeDtypeStruct((s, GDN_W), BF16),
                   jax.ShapeDtypeStruct((nc, 1, LANE), F32)],
        scratch_shapes=[pltpu.VMEM((halo + c, GDN_W), F32)],
        compiler_params=_cparams("parallel"),
        name="gdn_prep",
    )(p, p, p, p, p, p, conv_w, p, a_row, b_row)


def _gdn_scan_kernel(u_ref, w_ref, qh_ref, ktt_ref, intra_ref, eg_ref, z_ref, ng_ref, o_ref,
                     state_ref):
    @pl.when(pl.program_id(0) == 0)
    def _():
        state_ref[...] = jnp.zeros_like(state_ref)

    eg = eg_ref[...]
    for h in range(GDN_HEADS):
        hs = slice(h * LANE, (h + 1) * LANE)
        state = state_ref[h]
        state_b = state.astype(BF16)
        v_new = u_ref[:, hs] - _dot(w_ref[:, hs], state_b)
        v_new_b = v_new.astype(BF16)
        o = _dot(qh_ref[:, hs], state_b) + _dot(intra_ref[:, hs], v_new_b)
        state_ref[h] = state * eg[:, GA_LANE + h:GA_LANE + h + 1] + _dot(ktt_ref[:, hs], v_new_b)
        o = o * lax.rsqrt(jnp.mean(o * o, axis=-1, keepdims=True) + RMS_EPS) * ng_ref[...]
        o_ref[:, hs] = (o * _silu(z_ref[:, hs])).astype(o_ref.dtype)


def _gdn_scan(p, u, w, qh, ktt, intra, eg, norm_g):
    s = p.shape[0]
    c = GDN_CHUNK
    tok = pl.BlockSpec((c, GDN_W), lambda n: (n, 0))
    return pl.pallas_call(
        _gdn_scan_kernel,
        grid=(s // c,),
        in_specs=[tok, tok, tok, tok, tok,
                  pl.BlockSpec((None, 1, LANE), lambda n: (n, 0, 0)),
                  pl.BlockSpec((c, GDN_W), lambda n: (n, COL_GZ // GDN_W)),
                  pl.BlockSpec((1, LANE), lambda n: (0, 0))],
        out_specs=tok,
        out_shape=jax.ShapeDtypeStruct((s, GDN_W), BF16),
        scratch_shapes=[pltpu.VMEM((GDN_HEADS, HEAD_DIM, HEAD_DIM), F32)],
        compiler_params=_cparams("arbitrary"),
        name="gdn_scan",
    )(u, w, qh, ktt, intra, eg, p, norm_g)


def _sb_kernel(q_ref, k_ref, v_ref, z_ref, o_ref):
    i = pl.program_id(1)
    tq = q_ref.shape[0]
    q = q_ref[...]
    t_col = i * tq + lax.broadcasted_iota(jnp.int32, (tq, 1), 0)
    later = (lax.broadcasted_iota(jnp.int32, (LANE, LANE), 0)
             > lax.broadcasted_iota(jnp.int32, (LANE, LANE), 1)).astype(BF16)

    def tile_step(m, carry, on_diagonal):
        tail, acc = carry
        start = pl.multiple_of(m * LANE, LANE)
        k = k_ref[pl.ds(start, LANE), :]
        v = v_ref[pl.ds(start, LANE), :]
        z = _dot_nt(q, k) * ATTN_SCALE
        log_beta = jnp.minimum(z, 0.0) - jnp.log(1.0 + jnp.exp(-jnp.abs(z)))
        log_1m = log_beta - z
        if on_diagonal:
            past = (start + lax.broadcasted_iota(jnp.int32, (1, LANE), 1)) < t_col
            log_1m = jnp.where(past, log_1m, 0.0)
        a = jnp.exp(log_beta + _dot_split_rhs(log_1m, later) + tail)
        if on_diagonal:
            a = jnp.where(past, a, 0.0)
        acc = acc + _dot(a.astype(BF16), v)
        return tail + jnp.sum(log_1m, axis=-1, keepdims=True), acc

    diag_tiles = tq // LANE
    top = (i + 1) * diag_tiles - 1
    carry = (jnp.zeros((tq, 1), F32), jnp.zeros((tq, HEAD_DIM), F32))
    tail, acc = lax.fori_loop(0, diag_tiles, lambda it, c: tile_step(top - it, c, True), carry)

    def live(state):
        m, tail, _ = state
        return (m >= 0) & (jnp.max(tail) >= SB_UNDERFLOW)

    def older(state):
        m, tail, acc = state
        tail, acc = tile_step(m, (tail, acc), False)
        return m - 1, tail, acc

    _, _, acc = lax.while_loop(live, older, (i * diag_tiles - 1, tail, acc))
    o_ref[...] = (acc * _silu(z_ref[...])).astype(o_ref.dtype)


def _stick_breaking(p, pb):
    s = p.shape[0]
    tq = min(SB_TQ, s)
    return pl.pallas_call(
        _sb_kernel,
        grid=(SB_HEADS, s // tq),
        in_specs=[pl.BlockSpec((tq, LANE), lambda h, i: (i, COL_SQ // LANE + h)),
                  pl.BlockSpec((s, LANE), lambda h, i: (0, COL_SK // LANE + h)),
                  pl.BlockSpec((s, LANE), lambda h, i: (0, COL_SV // LANE + h)),
                  pl.BlockSpec((tq, LANE), lambda h, i: (i, COL_SZ // LANE + h))],
        out_specs=pl.BlockSpec((tq, LANE), lambda h, i: (i, h)),
        out_shape=jax.ShapeDtypeStruct((s, SB_W), BF16),
        compiler_params=_cparams("parallel", "parallel"),
        name="stick_breaking",
    )(pb, pb, pb, p)


def _outproj_kernel(alpha, yn_ref, yg_ref, ys_ref, x_ref, w_ref, gate_ref, g_ref, b_ref, o_ref):
    y = (_dot(yn_ref[...], w_ref[0:NSA_W, :])
         + _dot(yg_ref[...], w_ref[NSA_W:NSA_W + GDN_W, :])
         + _dot(ys_ref[...], w_ref[NSA_W + GDN_W:MIX_W, :]))
    r = alpha * x_ref[...] + (1.0 + gate_ref[...]) * y
    mu = jnp.mean(r, axis=-1, keepdims=True)
    var = jnp.mean(jnp.square(r - mu), axis=-1, keepdims=True)
    o_ref[...] = (r - mu) * lax.rsqrt(var + LN_EPS) * g_ref[...] + b_ref[...]


def _out_projection(y_nsa, y_gdn, y_sb, x, w_out_b, gate, ln_g, ln_b, alpha):
    s, d = x.shape
    tm = min(256, s)
    row = lambda i: (i, 0)
    const = lambda i: (0, 0)
    return pl.pallas_call(
        functools.partial(_outproj_kernel, alpha),
        grid=(s // tm,),
        in_specs=[pl.BlockSpec((tm, NSA_W), row), pl.BlockSpec((tm, GDN_W), row),
                  pl.BlockSpec((tm, SB_W), row), pl.BlockSpec((tm, d), row),
                  pl.BlockSpec((MIX_W, d), const), pl.BlockSpec((1, d), const),
                  pl.BlockSpec((1, d), const), pl.BlockSpec((1, d), const)],
        out_specs=pl.BlockSpec((tm, d), row),
        out_shape=jax.ShapeDtypeStruct((s, d), F32),
        compiler_params=_cparams("parallel"),
        name="out_projection",
    )(y_nsa, y_gdn, y_sb, x, w_out_b, gate, ln_g, ln_b)


def _segment_tables():
    nsa_qkv_w = NSA_W + 6 * KV_W
    segments = []
    o = 0
    for width, dst in ((nsa_qkv_w, COL_NQ), (3 * NSA_HEADS, COL_NG), (NSA_W, COL_NZ),
                       (3 * GDN_W, COL_GQ), (2 * GDN_HEADS, COL_GBA), (GDN_W, COL_GZ),
                       (4 * SB_W, COL_SQ)):
        segments.append((o, width, dst))
        o += width
    src = np.zeros(P_COLS // LANE, np.int32)
    shift = np.zeros_like(src)
    width_tab = np.zeros_like(src)
    for start, width, dst in segments:
        for off in range(0, width, LANE):
            j = (dst + off) // LANE
            src[j] = (start + off) // LANE
            shift[j] = (start + off) % LANE
            width_tab[j] = min(LANE, width - off)
    return src, shift, width_tab, o


def _wprep_kernel(row_ref, width_ref, w_ref, o_ref):
    j = pl.program_id(0)
    valid = lax.broadcasted_iota(jnp.int32, (LANE, 1), 0) < width_ref[j]
    for l in range(o_ref.shape[0]):
        o_ref[l] = jnp.where(valid, w_ref[:, l, :], 0.0).astype(BF16)


def _aligned_w_in(w_in):
    depth, d, n = w_in.shape
    src, shift, width_tab, total = _segment_tables()
    assert total == n
    row_start = src * LANE + shift
    w_t = jnp.transpose(w_in, (2, 0, 1))
    grid_spec = pltpu.PrefetchScalarGridSpec(
        num_scalar_prefetch=2,
        grid=(P_COLS // LANE,),
        in_specs=[pl.BlockSpec((pl.Element(LANE), pl.Element(depth), pl.Element(d)),
                               lambda j, rows, _: (rows[j], 0, 0))],
        out_specs=pl.BlockSpec((depth, LANE, d), lambda j, *_: (0, j, 0)))
    return pl.pallas_call(
        _wprep_kernel,
        grid_spec=grid_spec,
        out_shape=jax.ShapeDtypeStruct((depth, P_COLS, d), BF16),
        compiler_params=_cparams("parallel"),
        name="w_in_layout",
    )(jnp.asarray(row_start), jnp.asarray(width_tab), w_t)


def _rope_tables(pos):
    inv = ROPE_THETA ** (-jnp.arange(ROPE_HALF, dtype=F32) * 2.0 / ROPE_DIM)
    ang = pos.astype(F32)[:, None] * inv
    cos, sin = jnp.cos(ang), jnp.sin(ang)
    n = pos.shape[0]
    rest = HEAD_DIM - ROPE_DIM
    cos_t = jnp.concatenate([cos, cos, jnp.ones((n, rest), F32)], axis=1)
    sin_a = jnp.concatenate([-sin, jnp.zeros((n, HEAD_DIM - ROPE_HALF), F32)], axis=1)
    sin_b = jnp.concatenate([jnp.zeros((n, ROPE_HALF), F32), sin, jnp.zeros((n, rest), F32)], axis=1)
    return cos_t, sin_a, sin_b


def kernel(x, c, positions, w_mod, b_mod, w_in, w_out, ln_g, ln_b, cmp_pe_k, cmp_pe_v, cmp_w1_k, cmp_w2_k, cmp_w1_v, cmp_w2_v, gdn_conv_w, gdn_a_log, gdn_dt_bias, gdn_norm_g):
    b, s, d = x.shape
    assert b == 1 and s % (8 * Q_BLOCK) == 0 and s >= WINDOW + Q_BLOCK
    depth = w_mod.shape[0]
    alpha = (2 * depth) ** 0.25
    xs = x[0]
    pos = positions[0]

    cos_t, sin_a, sin_b = _rope_tables(pos)
    n_rows = s // CMP_STRIDE
    cmp_end = jnp.minimum(jnp.arange(n_rows) * CMP_STRIDE + CMP_LEN - 1, s - 1)
    cos_c, sa_c, sb_c = _rope_tables(pos[cmp_end])

    mod = _modulation(c, w_mod, b_mod)
    lane_pad = lambda v: jnp.pad(v, (GA_LANE, LANE - GA_LANE - v.shape[0])).reshape(1, LANE)

    w_in_b = _aligned_w_in(w_in)

    for l in range(depth):
        shift, scale, gate = (mod[l, :, k * d:(k + 1) * d] for k in range(3))
        p, pb = _in_projection(xs, shift, scale, w_in_b, l, cos_t, sin_a, sin_b)

        kc, vc = _compress(p, cmp_pe_k[l], cmp_pe_v[l], cmp_w1_k[l], cmp_w2_k[l],
                           cmp_w1_v[l], cmp_w2_v[l], cos_c, sa_c, sb_c)
        y_nsa = _nsa_attention(p, pb, kc, vc)

        a_row = lane_pad(-jnp.exp(gdn_a_log[l].astype(F32)))
        b_row = lane_pad(gdn_dt_bias[l].astype(F32))
        u, w, qh, ktt, intra, eg = _gdn_prep(p, gdn_conv_w[l], a_row, b_row)
        y_gdn = _gdn_scan(p, u, w, qh, ktt, intra, eg, gdn_norm_g[l].reshape(1, LANE))

        y_sb = _stick_breaking(p, pb)

        xs = _out_projection(y_nsa, y_gdn, y_sb, xs, w_out[l].astype(BF16), gate,
                             ln_g[l].reshape(1, d), ln_b[l].reshape(1, d), alpha)
    return xs[None]
```

```python
import functools
import math

import jax
import jax.numpy as jnp
import numpy as np
from jax import lax
from jax.experimental import pallas as pl
from jax.experimental.pallas import tpu as pltpu

F32 = jnp.float32
BF16 = jnp.bfloat16

LANE = 128
HEAD_DIM = 128
ROPE_DIM = HEAD_DIM // 4
ROPE_HALF = ROPE_DIM // 2
ROPE_THETA = 500000.0
Q_BLOCK = 128

NSA_HEADS = 6
NSA_KV_HEADS = 2
NSA_GROUP = NSA_HEADS // NSA_KV_HEADS
CMP_LEN = 32
CMP_STRIDE = 16
SLC_LEN = 64
SLC_TOPK = 16
WINDOW = 512
FORCE_BONUS = 1.0e4
GDN_HEADS = 6
GDN_CONV = 4
SB_HEADS = 4

NSA_W = NSA_HEADS * HEAD_DIM
KV_W = NSA_KV_HEADS * HEAD_DIM
GDN_W = GDN_HEADS * HEAD_DIM
SB_W = SB_HEADS * HEAD_DIM
MIX_W = NSA_W + GDN_W + SB_W

LN_EPS = 1e-5
RMS_EPS = 1e-6
NEG_INF = -1e30
ATTN_SCALE = HEAD_DIM ** -0.5
EXP2_SCALE = ATTN_SCALE * math.log2(math.e)

COL_NQ = 0
COL_NKC = COL_NQ + NSA_W
COL_NVC = COL_NKC + KV_W
COL_NKS = COL_NVC + KV_W
COL_NVS = COL_NKS + KV_W
COL_NKW = COL_NVS + KV_W
COL_NVW = COL_NKW + KV_W
COL_NZ = COL_NVW + KV_W
COL_GQ = COL_NZ + NSA_W
COL_GK = COL_GQ + GDN_W
COL_GV = COL_GK + GDN_W
COL_GZ = COL_GV + GDN_W
COL_SQ = COL_GZ + GDN_W
COL_SK = COL_SQ + SB_W
COL_SV = COL_SK + SB_W
COL_SZ = COL_SV + SB_W
COL_NG = COL_SZ + SB_W
COL_GBA = COL_NG + LANE
P_COLS = COL_GBA + LANE
GA_LANE = GDN_HEADS

GDN_CHUNK = 128
PROJ_TN = 768
SLC_CHUNK = 512
INV_SPLIT_LEVELS = 2
SB_TQ = 512
SB_UNDERFLOW = -104.0
VMEM_LIMIT = 56 * 1024 * 1024


def _cparams(*sem):
    return pltpu.CompilerParams(dimension_semantics=sem, vmem_limit_bytes=VMEM_LIMIT)


def _dot(a, b):
    return jnp.dot(a, b, preferred_element_type=F32)


def _dot_nt(a, b):
    return lax.dot_general(a, b, (((1,), (1,)), ((), ())), preferred_element_type=F32)


def _split3(x):
    hi = x.astype(BF16)
    r1 = x - hi.astype(F32)
    mid = r1.astype(BF16)
    lo = (r1 - mid.astype(F32)).astype(BF16)
    return hi, mid, lo


def _dot_split_rhs(x, m_bf16):
    hi = x.astype(BF16)
    lo = (x - hi.astype(F32)).astype(BF16)
    return _dot(hi, m_bf16) + _dot(lo, m_bf16)


def _dot_exact_lhs(m_bf16, x):
    hi, mid, lo = _split3(x)
    return _dot(m_bf16, hi) + _dot(m_bf16, mid) + _dot(m_bf16, lo)


def _dot_hi(a, b):
    a_hi = a.astype(BF16)
    a_lo = (a - a_hi.astype(F32)).astype(BF16)
    b_hi = b.astype(BF16)
    b_lo = (b - b_hi.astype(F32)).astype(BF16)
    return _dot(a_hi, b_hi) + _dot(a_hi, b_lo) + _dot(a_lo, b_hi)


def _silu(x):
    return x * jax.nn.sigmoid(x)


def _rope_tile(x, cos_t, sin_a, sin_b):
    return (x * cos_t + pltpu.roll(x, LANE - ROPE_HALF, 1) * sin_a
            + pltpu.roll(x, ROPE_HALF, 1) * sin_b)


def _mod_kernel(c_ref, w_ref, b_ref, o_ref):
    c_act = _silu(c_ref[...])
    o_ref[...] = jnp.sum(c_act * w_ref[...], axis=0, keepdims=True) + b_ref[...]


def _modulation(c, w_mod, b_mod):
    depth, d, n = w_mod.shape
    tn = 512
    return pl.pallas_call(
        _mod_kernel,
        grid=(depth, n // tn),
        in_specs=[pl.BlockSpec((d, 1), lambda l, j: (0, 0)),
                  pl.BlockSpec((None, d, tn), lambda l, j: (l, 0, j)),
                  pl.BlockSpec((None, 1, tn), lambda l, j: (l, 0, j))],
        out_specs=pl.BlockSpec((None, 1, tn), lambda l, j: (l, 0, j)),
        out_shape=jax.ShapeDtypeStruct((depth, 1, n), F32),
        compiler_params=_cparams("parallel", "parallel"),
        name="adaln_mod",
    )(c.reshape(d, 1), w_mod, b_mod.reshape(depth, 1, n))


ROPE_TILES = frozenset(range(COL_NQ // LANE, COL_NKC // LANE)) | frozenset(
    range(COL_NKS // LANE, COL_NVS // LANE)) | frozenset(range(COL_NKW // LANE, COL_NVW // LANE))


def _inproj_kernel(x_ref, shift_ref, scale_ref, w_ref, cos_ref, sa_ref, sb_ref, o32_ref, o16_ref,
                   h_ref):
    j = pl.program_id(1)

    def emit(cols, tile):
        o32_ref[:, cols] = tile
        o16_ref[:, cols] = tile.astype(BF16)


    @pl.when(j == 0)
    def _():
        h_ref[...] = (x_ref[...] * (1.0 + scale_ref[...]) + shift_ref[...]).astype(BF16)

    acc = _dot_nt(h_ref[...], w_ref[...])
    tiles_per_step = PROJ_TN // LANE
    rope_steps = sorted({t // tiles_per_step for t in ROPE_TILES})
    plain = j >= 0
    for step in rope_steps:
        plain = plain & (j != step)

        @pl.when(j == step)
        def _(step=step):
            cos_t, sin_a, sin_b = cos_ref[...], sa_ref[...], sb_ref[...]
            for t in range(tiles_per_step):
                tile = acc[:, t * LANE:(t + 1) * LANE]
                if step * tiles_per_step + t in ROPE_TILES:
                    tile = _rope_tile(tile, cos_t, sin_a, sin_b)
                emit(slice(t * LANE, (t + 1) * LANE), tile)

    @pl.when(plain)
    def _():
        emit(slice(None), acc)


def _in_projection(x, shift, scale, w_in_b, layer, cos_t, sin_a, sin_b):
    s, d = x.shape
    n = w_in_b.shape[1]
    tm = min(1024, s)
    return pl.pallas_call(
        _inproj_kernel,
        grid=(s // tm, n // PROJ_TN),
        in_specs=[pl.BlockSpec((tm, d), lambda i, j: (i, 0)),
                  pl.BlockSpec((1, d), lambda i, j: (0, 0)),
                  pl.BlockSpec((1, d), lambda i, j: (0, 0)),
                  pl.BlockSpec((None, PROJ_TN, d), lambda i, j: (layer, j, 0)),
                  pl.BlockSpec((tm, LANE), lambda i, j: (i, 0)),
                  pl.BlockSpec((tm, LANE), lambda i, j: (i, 0)),
                  pl.BlockSpec((tm, LANE), lambda i, j: (i, 0))],
        out_specs=[pl.BlockSpec((tm, PROJ_TN), lambda i, j: (i, j)),
                   pl.BlockSpec((tm, PROJ_TN), lambda i, j: (i, j))],
        out_shape=[jax.ShapeDtypeStruct((s, n), F32), jax.ShapeDtypeStruct((s, n), BF16)],
        scratch_shapes=[pltpu.VMEM((tm, d), BF16)],
        compiler_params=_cparams("parallel", "arbitrary"),
        name="in_projection",
    )(x, shift, scale, w_in_b, cos_t, sin_a, sin_b)


def _compress_kernel(k_ref, v_ref, pek_lo, pek_hi, pev_lo, pev_hi, wk_lo, wk_hi, wv_lo, wv_hi,
                     w2k_ref, w2v_ref, cos_ref, sa_ref, sb_ref, kc_ref, vc_ref,
                     ak_ref, bk_ref, av_ref, bv_ref):
    l = pl.program_id(0)

    @pl.when(l == 0)
    def _():
        for r in (ak_ref, bk_ref, av_ref, bv_ref):
            r[...] = jnp.zeros_like(r)

    kx = k_ref[...]
    vx = v_ref[...]
    ak_ref[...] += _dot((kx + pek_lo[...]).astype(BF16), wk_lo[...])
    bk_ref[...] += _dot((kx + pek_hi[...]).astype(BF16), wk_hi[...])
    av_ref[...] += _dot((vx + pev_lo[...]).astype(BF16), wv_lo[...])
    bv_ref[...] += _dot((vx + pev_hi[...]).astype(BF16), wv_hi[...])

    @pl.when(l == CMP_STRIDE - 1)
    def _():
        rows = ak_ref.shape[0]
        hk = ak_ref[...] + pltpu.roll(bk_ref[...], rows - 1, 0)
        hv = av_ref[...] + pltpu.roll(bv_ref[...], rows - 1, 0)
        kc = _dot(_silu(hk).astype(BF16), w2k_ref[...])
        vc = _dot(_silu(hv).astype(BF16), w2v_ref[...])
        cos_t, sin_a, sin_b = cos_ref[...], sa_ref[...], sb_ref[...]
        for g in range(NSA_KV_HEADS):
            sl = slice(g * LANE, (g + 1) * LANE)
            kc_ref[:, sl] = _rope_tile(kc[:, sl], cos_t, sin_a, sin_b).astype(BF16)
        vc_ref[...] = vc.astype(BF16)


def _block_diag2(w):
    z = jnp.zeros_like(w)
    return jnp.concatenate([jnp.concatenate([w, z], axis=-1),
                            jnp.concatenate([z, w], axis=-1)], axis=-2)


def _compress(p, pe_k, pe_v, w1_k, w2_k, w1_v, w2_v, cos_c, sa_c, sb_c):
    s = p.shape[0]
    rows = s // CMP_STRIDE
    assert COL_NVC == COL_NKC + KV_W
    kv_cols = lax.slice(p, (0, COL_NKC), (s, COL_NKC + 2 * KV_W))
    pg = kv_cols.reshape(rows, CMP_STRIDE * 2 * KV_W)
    blocks_per_tok = 2
    kcol, vcol = 0, 1

    def w1_parts(w1):
        w = w1.reshape(CMP_LEN, HEAD_DIM, HEAD_DIM).astype(BF16)
        return _block_diag2(w[:CMP_STRIDE]), _block_diag2(w[CMP_STRIDE:])

    def pe_parts(pe):
        t = jnp.concatenate([pe, pe], axis=-1).reshape(CMP_LEN, 1, KV_W)
        return t[:CMP_STRIDE], t[CMP_STRIDE:]

    wk_lo, wk_hi = w1_parts(w1_k)
    wv_lo, wv_hi = w1_parts(w1_v)
    pek_lo, pek_hi = pe_parts(pe_k)
    pev_lo, pev_hi = pe_parts(pe_v)
    w2k = _block_diag2(w2_k.astype(BF16))
    w2v = _block_diag2(w2_v.astype(BF16))

    tok_spec = lambda col: pl.BlockSpec((rows, KV_W), lambda l: (0, l * blocks_per_tok + col))
    pe_spec = pl.BlockSpec((None, 1, KV_W), lambda l: (l, 0, 0))
    w1_spec = pl.BlockSpec((None, KV_W, KV_W), lambda l: (l, 0, 0))
    full = lambda a: pl.BlockSpec(a.shape, lambda l: (0,) * a.ndim)
    return pl.pallas_call(
        _compress_kernel,
        grid=(CMP_STRIDE,),
        in_specs=[tok_spec(kcol), tok_spec(vcol), pe_spec, pe_spec, pe_spec, pe_spec,
                  w1_spec, w1_spec, w1_spec, w1_spec, full(w2k), full(w2v),
                  full(cos_c), full(sa_c), full(sb_c)],
        out_specs=[pl.BlockSpec((rows, KV_W), lambda l: (0, 0)),
                   pl.BlockSpec((rows, KV_W), lambda l: (0, 0))],
        out_shape=[jax.ShapeDtypeStruct((rows, KV_W), BF16),
                   jax.ShapeDtypeStruct((rows, KV_W), BF16)],
        scratch_shapes=[pltpu.VMEM((rows, KV_W), F32)] * 4,
        compiler_params=_cparams("arbitrary"),
        name="nsa_compress",
    )(pg, pg, pek_lo, pek_hi, pev_lo, pev_hi, wk_lo, wk_hi, wv_lo, wv_hi, w2k, w2v,
      cos_c, sa_c, sb_c)


def _softmax_rows(s, mask):
    s = jnp.where(mask, s, NEG_INF)
    m = jnp.max(s, axis=-1, keepdims=True)
    e = jnp.where(mask, jnp.exp(s - m), 0.0)
    return e / jnp.maximum(jnp.sum(e, axis=-1, keepdims=True), 1e-30)


def _select_top_blocks(imps):
    imp_ts = [imp.T for imp in imps]
    nb = imp_ts[0].shape[0]
    blk = lax.broadcasted_iota(jnp.int32, imp_ts[0].shape, 0).astype(F32)

    def pick_one(_, carry):
        out = []
        for vals, sel in carry:
            best = jnp.max(vals, axis=0, keepdims=True)
            first = jnp.min(jnp.where(vals == best, blk, float(nb)), axis=0, keepdims=True)
            hit = blk == first
            out.append((jnp.where(hit, -jnp.inf, vals), jnp.where(hit, 1.0, sel)))
        return tuple(out)

    done = lax.fori_loop(0, min(SLC_TOPK, nb), pick_one,
                         tuple((imp_t, jnp.zeros_like(imp_t)) for imp_t in imp_ts))
    return [sel_t for _, sel_t in done]


def _nsa_kernel(q_ref, g_ref, z_ref, kc_ref, vc_ref, ov_ref, ks_ref, vst_ref, kw_ref, vw_ref,
                o_ref):
    i = pl.program_id(0)
    tq = q_ref.shape[0]
    n_cmp = kc_ref.shape[0]
    n_slc = ov_ref.shape[1]
    seq = ks_ref.shape[0]
    rows = NSA_GROUP * tq

    t_col = i * tq + lax.broadcasted_iota(jnp.int32, (tq, 1), 0)
    t_rows = jnp.concatenate([t_col] * NSA_GROUP, axis=0)
    gates = jax.nn.sigmoid(g_ref[...])

    cmp_end = lax.broadcasted_iota(jnp.int32, (1, n_cmp), 1) * CMP_STRIDE + (CMP_LEN - 1)
    cmp_mask = cmp_end <= t_rows
    slc_id = lax.broadcasted_iota(jnp.int32, (1, n_slc), 1)
    cur = t_col // SLC_LEN
    future = slc_id * SLC_LEN > t_col
    forced = (slc_id == 0) | (slc_id == cur) | (slc_id == cur - 1)
    t_row = i * tq + lax.broadcasted_iota(jnp.int32, (1, tq), 1)
    future_t = lax.broadcasted_iota(jnp.int32, (n_slc, 1), 0) * SLC_LEN > t_row

    win_len = WINDOW + tq
    win_start = pl.multiple_of(jnp.clip(i * tq - WINDOW, 0, seq - win_len), LANE)
    win_pos = win_start + lax.broadcasted_iota(jnp.int32, (1, win_len), 1)
    dpos = t_rows - win_pos
    win_mask = (dpos >= 0) & (dpos < WINDOW)

    groups = range(NSA_KV_HEADS)
    kv_cols = [slice(g * LANE, (g + 1) * LANE) for g in groups]
    heads_of = [[g * NSA_GROUP + r for r in range(NSA_GROUP)] for g in groups]
    qs, o_cs, imps = [], [], []
    for g in groups:
        q = jnp.concatenate([q_ref[:, h * LANE:(h + 1) * LANE] for h in heads_of[g]],
                            axis=0)
        pc = _softmax_rows(_dot_nt(q, kc_ref[:, kv_cols[g]]) * ATTN_SCALE, cmp_mask)
        pc_b = pc.astype(BF16)
        o_cs.append(_dot(pc_b, vc_ref[:, kv_cols[g]]))
        imp = _dot(pc_b[0:tq], ov_ref[...])
        for r in range(1, NSA_GROUP):
            imp = imp + _dot(pc_b[r * tq:(r + 1) * tq], ov_ref[...])
        imps.append(jnp.where(future, NEG_INF, jnp.where(forced, imp + FORCE_BONUS, imp)))
        qs.append(q)
    sels = [jnp.where(future_t, 0.0, sel_t).astype(BF16) for sel_t in _select_top_blocks(imps)]

    key_blk = lax.broadcasted_iota(jnp.int32, (SLC_CHUNK, n_slc), 0) // SLC_LEN
    blk_id = lax.broadcasted_iota(jnp.int32, (SLC_CHUNK, n_slc), 1)
    key_off = lax.broadcasted_iota(jnp.int32, (SLC_CHUNK, 1), 0)

    def slc_step(c, carry):
        start = pl.multiple_of(c * SLC_CHUNK, SLC_CHUNK)
        causal = (start + key_off) <= t_row
        expand = (blk_id == key_blk + c * (SLC_CHUNK // SLC_LEN)).astype(BF16)
        out = []
        for g in groups:
            m_old, l_old, acc = carry[g]
            k = ks_ref[pl.ds(start, SLC_CHUNK), kv_cols[g]]
            v_t = vst_ref[c, kv_cols[g], :]
            picked = _dot(expand, sels[g])
            bias = jnp.where((picked > 0.5) & causal, 0.0, NEG_INF)
            s = _dot_nt(k, qs[g]) * EXP2_SCALE + jnp.concatenate([bias] * NSA_GROUP, axis=1)
            m_new = jnp.maximum(m_old, jnp.max(s, axis=0, keepdims=True))
            p = jnp.exp2(s - m_new)
            alpha = jnp.exp2(m_old - m_new)
            out.append((m_new, alpha * l_old + jnp.sum(p, axis=0, keepdims=True),
                        alpha * acc + _dot(v_t, p.astype(BF16))))
        return tuple(out)

    init = tuple((jnp.full((1, rows), NEG_INF, F32), jnp.zeros((1, rows), F32),
                  jnp.zeros((HEAD_DIM, rows), F32)) for _ in groups)
    slc = lax.fori_loop(0, (i * tq) // SLC_CHUNK + 1, slc_step, init)

    for g in groups:
        _, l_s, acc_s = slc[g]
        o_s_t = acc_s / jnp.maximum(l_s, 1e-30)
        o_s = jnp.concatenate([o_s_t[:, r * tq:(r + 1) * tq].T for r in range(NSA_GROUP)],
                              axis=0)
        kw = kw_ref[pl.ds(win_start, win_len), kv_cols[g]]
        vw = vw_ref[pl.ds(win_start, win_len), kv_cols[g]]
        pw = _softmax_rows(_dot_nt(qs[g], kw) * ATTN_SCALE, win_mask)
        o_w = _dot(pw.astype(BF16), vw)
        o_c = o_cs[g]
        for r, h in enumerate(heads_of[g]):
            rs = slice(r * tq, (r + 1) * tq)
            mix = (gates[:, 3 * h:3 * h + 1] * o_c[rs] + gates[:, 3 * h + 1:3 * h + 2] * o_s[rs]
                   + gates[:, 3 * h + 2:3 * h + 3] * o_w[rs])
            hs = slice(h * LANE, (h + 1) * LANE)
            o_ref[:, hs] = (mix * _silu(z_ref[:, hs])).astype(o_ref.dtype)


def _overlap_matrix(n_cmp_rows, n_cmp, n_slc):
    starts = np.arange(n_cmp_rows) * CMP_STRIDE
    ends = starts + CMP_LEN
    blk = np.arange(n_slc)
    ov = np.minimum(ends[:, None], (blk[None, :] + 1) * SLC_LEN) - np.maximum(
        starts[:, None], blk[None, :] * SLC_LEN)
    ov = np.maximum(ov, 0).astype(np.float32) / CMP_LEN
    ov[n_cmp:] = 0.0
    return ov


def _nsa_attention(p, pb, kc, vc):
    s = p.shape[0]
    tq = Q_BLOCK
    n_cmp_rows = kc.shape[0]
    n_cmp = (s - CMP_LEN) // CMP_STRIDE + 1
    n_slc = s // SLC_LEN
    ov = jnp.asarray(_overlap_matrix(n_cmp_rows, n_cmp, n_slc), BF16)
    v_slc = lax.slice(pb, (0, COL_NVS), (s, COL_NVS + KV_W))
    v_slc_t = jnp.transpose(v_slc.reshape(s // SLC_CHUNK, SLC_CHUNK, KV_W), (0, 2, 1))
    full = lambda a: pl.BlockSpec(a.shape, lambda i: (0,) * a.ndim)
    kv_spec = lambda col: pl.BlockSpec((s, KV_W), lambda i: (0, col // KV_W))
    return pl.pallas_call(
        _nsa_kernel,
        grid=(s // tq,),
        in_specs=[pl.BlockSpec((tq, NSA_W), lambda i: (i, COL_NQ // NSA_W)),
                  pl.BlockSpec((tq, LANE), lambda i: (i, COL_NG // LANE)),
                  pl.BlockSpec((tq, NSA_W), lambda i: (i, COL_NZ // NSA_W)),
                  full(kc), full(vc), full(ov),
                  kv_spec(COL_NKS), full(v_slc_t), kv_spec(COL_NKW), kv_spec(COL_NVW)],
        out_specs=pl.BlockSpec((tq, NSA_W), lambda i: (i, 0)),
        out_shape=jax.ShapeDtypeStruct((s, NSA_W), BF16),
        compiler_params=_cparams("parallel"),
        name="nsa_attention",
    )(pb, p, p, kc, vc, ov, pb, v_slc_t, pb, pb)


def _unit_lower_inverses(lows):
    n = lows[0].shape[0]
    eye = (lax.broadcasted_iota(jnp.int32, (n, n), 0)
           == lax.broadcasted_iota(jnp.int32, (n, n), 1)).astype(F32)
    invs = [eye - low for low in lows]
    powers = list(lows)
    for level in range(int(math.log2(n)) - 1):
        mm = _dot_hi if level < INV_SPLIT_LEVELS else (
            lambda a, b: _dot(a.astype(BF16), b.astype(BF16)))
        powers = [mm(pw, pw) for pw in powers]
        invs = [inv + mm(inv, pw) for inv, pw in zip(invs, powers)]
    return invs


def _gdn_prep_kernel(q_ref, k_ref, v_ref, qp_ref, kp_ref, vp_ref, cw_ref, gba_ref, arow_ref,
                     brow_ref, u_ref, w_ref, qh_ref, ktt_ref, intra_ref, eg_ref, xs_ref):
    n = pl.program_id(0)
    c = q_ref.shape[0]
    halo = qp_ref.shape[0]
    keep_halo = (n > 0).astype(F32)

    def conv_silu(cur_ref, prev_ref, col0):
        xs_ref[0:halo, :] = prev_ref[...] * keep_halo
        xs_ref[halo:halo + c, :] = cur_ref[...]
        y = jnp.zeros((c, GDN_W), F32)
        for tap in range(GDN_CONV):
            off = halo - (GDN_CONV - 1) + tap
            y = y + xs_ref[off:off + c, :] * cw_ref[tap:tap + 1, col0:col0 + GDN_W]
        return _silu(y)

    q_all = conv_silu(q_ref, qp_ref, 0)
    k_all = conv_silu(k_ref, kp_ref, GDN_W)
    v_all = conv_silu(v_ref, vp_ref, 2 * GDN_W)

    row = lax.broadcasted_iota(jnp.int32, (c, c), 0)
    col = lax.broadcasted_iota(jnp.int32, (c, c), 1)
    incl = row >= col
    strict = row > col
    tri = incl.astype(BF16)

    gba = gba_ref[...]
    beta_all = jax.nn.sigmoid(gba)
    x = gba + brow_ref[...]
    softplus = jnp.maximum(x, 0.0) + jnp.log1p(jnp.exp(-jnp.abs(x)))
    g_all = arow_ref[...] * softplus
    gc_all = _dot_exact_lhs(tri, g_all)
    eg_ref[...] = jnp.exp(gc_all[c - 1:c, :])

    lows, rhs_u, rhs_w = [], [], []
    for h in range(GDN_HEADS):
        hs = slice(h * LANE, (h + 1) * LANE)
        q = q_all[:, hs]
        k = k_all[:, hs]
        q = q * lax.rsqrt(jnp.sum(q * q, axis=-1, keepdims=True) + RMS_EPS)
        k = k * lax.rsqrt(jnp.sum(k * k, axis=-1, keepdims=True) + RMS_EPS)
        q = q * ATTN_SCALE
        beta = beta_all[:, h:h + 1]
        gc = gc_all[:, GA_LANE + h:GA_LANE + h + 1]
        gc_row = jnp.sum(jnp.where(row == col, gc, 0.0), axis=0, keepdims=True)
        g_last = gc[c - 1:c, :]
        decay = jnp.where(incl, jnp.exp(jnp.where(incl, gc - gc_row, 0.0)), 0.0)
        kb = k * beta
        k_b16 = k.astype(BF16)
        lows.append(jnp.where(strict, _dot_nt(kb.astype(BF16), k_b16) * decay, 0.0))
        rhs_u.append(v_all[:, hs] * beta)
        rhs_w.append(kb * jnp.exp(gc))
        intra_ref[:, hs] = (_dot_nt(q.astype(BF16), k_b16) * decay).astype(BF16)
        qh_ref[:, hs] = (q * jnp.exp(gc)).astype(BF16)
        ktt_ref[:, hs] = (k * jnp.exp(g_last - gc)).T.astype(BF16)

    for h, inv in enumerate(_unit_lower_inverses(lows)):
        hs = slice(h * LANE, (h + 1) * LANE)
        u_ref[:, hs] = _dot_hi(inv, rhs_u[h])
        w_ref[:, hs] = _dot_hi(inv, rhs_w[h]).astype(BF16)


def _gdn_prep(p, conv_w, a_row, b_row):
    s = p.shape[0]
    c = GDN_CHUNK
    halo = 8
    nc = s // c
    cur = lambda col: pl.BlockSpec((c, GDN_W), lambda n: (n, col // GDN_W))
    prev = lambda col: pl.BlockSpec(
        (halo, GDN_W), lambda n: (jnp.maximum(n * (c // halo) - 1, 0), col // GDN_W))
    full = lambda a: pl.BlockSpec(a.shape, lambda n: (0,) * a.ndim)
    tok_out = pl.BlockSpec((c, GDN_W), lambda n: (n, 0))
    return pl.pallas_call(
        _gdn_prep_kernel,
        grid=(nc,),
        in_specs=[cur(COL_GQ), cur(COL_GK), cur(COL_GV), prev(COL_GQ), prev(COL_GK), prev(COL_GV),
                  full(conv_w), pl.BlockSpec((c, LANE), lambda n: (n, COL_GBA // LANE)),
                  full(a_row), full(b_row)],
        out_specs=[tok_out, tok_out, tok_out, tok_out, tok_out,
                   pl.BlockSpec((None, 1, LANE), lambda n: (n, 0, 0))],
        out_shape=[jax.ShapeDtypeStruct((s, GDN_W), F32),
                   jax.ShapeDtypeStruct((s, GDN_W), BF16),
                   jax.ShapeDtypeStruct((s, GDN_W), BF16),
                   jax.ShapeDtypeStruct((s, GDN_W), BF16),
                   jax.ShapeDtypeStruct((s, GDN_W), BF16),
                   jax.ShapeDtypeStruct((nc, 1, LANE), F32)],
        scratch_shapes=[pltpu.VMEM((halo + c, GDN_W), F32)],
        compiler_params=_cparams("parallel"),
        name="gdn_prep",
    )(p, p, p, p, p, p, conv_w, p, a_row, b_row)


def _gdn_scan_kernel(u_ref, w_ref, qh_ref, ktt_ref, intra_ref, eg_ref, z_ref, ng_ref, o_ref,
                     state_ref):
    @pl.when(pl.program_id(0) == 0)
    def _():
        state_ref[...] = jnp.zeros_like(state_ref)

    eg = eg_ref[...]
    for h in range(GDN_HEADS):
        hs = slice(h * LANE, (h + 1) * LANE)
        state = state_ref[h]
        state_b = state.astype(BF16)
        v_new = u_ref[:, hs] - _dot(w_ref[:, hs], state_b)
        v_new_b = v_new.astype(BF16)
        o = _dot(qh_ref[:, hs], state_b) + _dot(intra_ref[:, hs], v_new_b)
        state_ref[h] = state * eg[:, GA_LANE + h:GA_LANE + h + 1] + _dot(ktt_ref[:, hs], v_new_b)
        o = o * lax.rsqrt(jnp.mean(o * o, axis=-1, keepdims=True) + RMS_EPS) * ng_ref[...]
        o_ref[:, hs] = (o * _silu(z_ref[:, hs])).astype(o_ref.dtype)


def _gdn_scan(p, u, w, qh, ktt, intra, eg, norm_g):
    s = p.shape[0]
    c = GDN_CHUNK
    tok = pl.BlockSpec((c, GDN_W), lambda n: (n, 0))
    return pl.pallas_call(
        _gdn_scan_kernel,
        grid=(s // c,),
        in_specs=[tok, tok, tok, tok, tok,
                  pl.BlockSpec((None, 1, LANE), lambda n: (n, 0, 0)),
                  pl.BlockSpec((c, GDN_W), lambda n: (n, COL_GZ // GDN_W)),
                  pl.BlockSpec((1, LANE), lambda n: (0, 0))],
        out_specs=tok,
        out_shape=jax.ShapeDtypeStruct((s, GDN_W), BF16),
        scratch_shapes=[pltpu.VMEM((GDN_HEADS, HEAD_DIM, HEAD_DIM), F32)],
        compiler_params=_cparams("arbitrary"),
        name="gdn_scan",
    )(u, w, qh, ktt, intra, eg, p, norm_g)


def _sb_kernel(q_ref, k_ref, v_ref, z_ref, o_ref):
    i = pl.program_id(1)
    tq = q_ref.shape[0]
    q = q_ref[...]
    t_col = i * tq + lax.broadcasted_iota(jnp.int32, (tq, 1), 0)
    later = (lax.broadcasted_iota(jnp.int32, (LANE, LANE), 0)
             > lax.broadcasted_iota(jnp.int32, (LANE, LANE), 1)).astype(BF16)

    def tile_step(m, carry, first_row=None):
        tail, acc = carry
        on_diagonal = first_row is not None
        rows = slice(first_row, None)
        start = pl.multiple_of(m * LANE, LANE)
        k = k_ref[pl.ds(start, LANE), :]
        v = v_ref[pl.ds(start, LANE), :]
        z = _dot_nt(q[rows], k) * ATTN_SCALE
        log_beta = jnp.minimum(z, 0.0) - jnp.log(1.0 + jnp.exp(-jnp.abs(z)))
        log_1m = log_beta - z
        if on_diagonal:
            past = (start + lax.broadcasted_iota(jnp.int32, (1, LANE), 1)) < t_col[rows]
            log_1m = jnp.where(past, log_1m, 0.0)
        a = jnp.exp(log_beta + _dot_split_rhs(log_1m, later) + tail[rows])
        if on_diagonal:
            a = jnp.where(past, a, 0.0)
        acc_rows = acc[rows] + _dot(a.astype(BF16), v)
        tail_rows = tail[rows] + jnp.sum(log_1m, axis=-1, keepdims=True)
        if not first_row:
            return tail_rows, acc_rows
        return (jnp.concatenate([tail[:first_row], tail_rows], axis=0),
                jnp.concatenate([acc[:first_row], acc_rows], axis=0))

    diag_tiles = tq // LANE
    top = (i + 1) * diag_tiles - 1
    carry = (jnp.zeros((tq, 1), F32), jnp.zeros((tq, HEAD_DIM), F32))
    for d in range(diag_tiles):
        carry = tile_step(top - d, carry, first_row=(diag_tiles - 1 - d) * LANE)
    tail, acc = carry

    def live(state):
        m, tail, _ = state
        return (m >= 0) & (jnp.max(tail) >= SB_UNDERFLOW)

    def older(state):
        m, tail, acc = state
        tail, acc = tile_step(m, (tail, acc))
        return m - 1, tail, acc

    _, _, acc = lax.while_loop(live, older, (i * diag_tiles - 1, tail, acc))
    o_ref[...] = (acc * _silu(z_ref[...])).astype(o_ref.dtype)


def _stick_breaking(p, pb):
    s = p.shape[0]
    tq = min(SB_TQ, s)
    return pl.pallas_call(
        _sb_kernel,
        grid=(SB_HEADS, s // tq),
        in_specs=[pl.BlockSpec((tq, LANE), lambda h, i: (i, COL_SQ // LANE + h)),
                  pl.BlockSpec((s, LANE), lambda h, i: (0, COL_SK // LANE + h)),
                  pl.BlockSpec((s, LANE), lambda h, i: (0, COL_SV // LANE + h)),
                  pl.BlockSpec((tq, LANE), lambda h, i: (i, COL_SZ // LANE + h))],
        out_specs=pl.BlockSpec((tq, LANE), lambda h, i: (i, h)),
        out_shape=jax.ShapeDtypeStruct((s, SB_W), BF16),
        compiler_params=_cparams("parallel", "parallel"),
        name="stick_breaking",
    )(pb, pb, pb, p)


def _outproj_kernel(alpha, yn_ref, yg_ref, ys_ref, x_ref, w_ref, gate_ref, g_ref, b_ref, o_ref):
    y = (_dot(yn_ref[...], w_ref[0:NSA_W, :])
         + _dot(yg_ref[...], w_ref[NSA_W:NSA_W + GDN_W, :])
         + _dot(ys_ref[...], w_ref[NSA_W + GDN_W:MIX_W, :]))
    r = alpha * x_ref[...] + (1.0 + gate_ref[...]) * y
    mu = jnp.mean(r, axis=-1, keepdims=True)
    var = jnp.mean(jnp.square(r - mu), axis=-1, keepdims=True)
    o_ref[...] = (r - mu) * lax.rsqrt(var + LN_EPS) * g_ref[...] + b_ref[...]


def _out_projection(y_nsa, y_gdn, y_sb, x, w_out_b, gate, ln_g, ln_b, alpha):
    s, d = x.shape
    tm = min(256, s)
    row = lambda i: (i, 0)
    const = lambda i: (0, 0)
    return pl.pallas_call(
        functools.partial(_outproj_kernel, alpha),
        grid=(s // tm,),
        in_specs=[pl.BlockSpec((tm, NSA_W), row), pl.BlockSpec((tm, GDN_W), row),
                  pl.BlockSpec((tm, SB_W), row), pl.BlockSpec((tm, d), row),
                  pl.BlockSpec((MIX_W, d), const), pl.BlockSpec((1, d), const),
                  pl.BlockSpec((1, d), const), pl.BlockSpec((1, d), const)],
        out_specs=pl.BlockSpec((tm, d), row),
        out_shape=jax.ShapeDtypeStruct((s, d), F32),
        compiler_params=_cparams("parallel"),
        name="out_projection",
    )(y_nsa, y_gdn, y_sb, x, w_out_b, gate, ln_g, ln_b)


def _segment_tables():
    nsa_qkv_w = NSA_W + 6 * KV_W
    segments = []
    o = 0
    for width, dst in ((nsa_qkv_w, COL_NQ), (3 * NSA_HEADS, COL_NG), (NSA_W, COL_NZ),
                       (3 * GDN_W, COL_GQ), (2 * GDN_HEADS, COL_GBA), (GDN_W, COL_GZ),
                       (4 * SB_W, COL_SQ)):
        segments.append((o, width, dst))
        o += width
    src = np.zeros(P_COLS // LANE, np.int32)
    shift = np.zeros_like(src)
    width_tab = np.zeros_like(src)
    for start, width, dst in segments:
        for off in range(0, width, LANE):
            j = (dst + off) // LANE
            src[j] = (start + off) // LANE
            shift[j] = (start + off) % LANE
            width_tab[j] = min(LANE, width - off)
    return src, shift, width_tab, o


def _wprep_kernel(row_ref, width_ref, w_ref, o_ref):
    j = pl.program_id(0)
    valid = lax.broadcasted_iota(jnp.int32, (LANE, 1), 0) < width_ref[j]
    for l in range(o_ref.shape[0]):
        o_ref[l] = jnp.where(valid, w_ref[:, l, :], 0.0).astype(BF16)


def _aligned_w_in(w_in):
    depth, d, n = w_in.shape
    src, shift, width_tab, total = _segment_tables()
    assert total == n
    row_start = src * LANE + shift
    w_t = jnp.transpose(w_in, (2, 0, 1))
    grid_spec = pltpu.PrefetchScalarGridSpec(
        num_scalar_prefetch=2,
        grid=(P_COLS // LANE,),
        in_specs=[pl.BlockSpec((pl.Element(LANE), pl.Element(depth), pl.Element(d)),
                               lambda j, rows, _: (rows[j], 0, 0))],
        out_specs=pl.BlockSpec((depth, LANE, d), lambda j, *_: (0, j, 0)))
    return pl.pallas_call(
        _wprep_kernel,
        grid_spec=grid_spec,
        out_shape=jax.ShapeDtypeStruct((depth, P_COLS, d), BF16),
        compiler_params=_cparams("parallel"),
        name="w_in_layout",
    )(jnp.asarray(row_start), jnp.asarray(width_tab), w_t)


def _rope_tables(pos):
    inv = ROPE_THETA ** (-jnp.arange(ROPE_HALF, dtype=F32) * 2.0 / ROPE_DIM)
    ang = pos.astype(F32)[:, None] * inv
    cos, sin = jnp.cos(ang), jnp.sin(ang)
    n = pos.shape[0]
    rest = HEAD_DIM - ROPE_DIM
    cos_t = jnp.concatenate([cos, cos, jnp.ones((n, rest), F32)], axis=1)
    sin_a = jnp.concatenate([-sin, jnp.zeros((n, HEAD_DIM - ROPE_HALF), F32)], axis=1)
    sin_b = jnp.concatenate([jnp.zeros((n, ROPE_HALF), F32), sin, jnp.zeros((n, rest), F32)], axis=1)
    return cos_t, sin_a, sin_b


def kernel(x, c, positions, w_mod, b_mod, w_in, w_out, ln_g, ln_b, cmp_pe_k, cmp_pe_v, cmp_w1_k, cmp_w2_k, cmp_w1_v, cmp_w2_v, gdn_conv_w, gdn_a_log, gdn_dt_bias, gdn_norm_g):
    b, s, d = x.shape
    assert b == 1 and s % (8 * Q_BLOCK) == 0 and s >= WINDOW + Q_BLOCK
    depth = w_mod.shape[0]
    alpha = (2 * depth) ** 0.25
    xs = x[0]
    pos = positions[0]

    cos_t, sin_a, sin_b = _rope_tables(pos)
    n_rows = s // CMP_STRIDE
    cmp_end = jnp.minimum(jnp.arange(n_rows) * CMP_STRIDE + CMP_LEN - 1, s - 1)
    cos_c, sa_c, sb_c = _rope_tables(pos[cmp_end])

    mod = _modulation(c, w_mod, b_mod)
    lane_pad = lambda v: jnp.pad(v, (GA_LANE, LANE - GA_LANE - v.shape[0])).reshape(1, LANE)

    w_in_b = _aligned_w_in(w_in)

    for l in range(depth):
        shift, scale, gate = (mod[l, :, k * d:(k + 1) * d] for k in range(3))
        p, pb = _in_projection(xs, shift, scale, w_in_b, l, cos_t, sin_a, sin_b)

        kc, vc = _compress(p, cmp_pe_k[l], cmp_pe_v[l], cmp_w1_k[l], cmp_w2_k[l],
                           cmp_w1_v[l], cmp_w2_v[l], cos_c, sa_c, sb_c)
        y_nsa = _nsa_attention(p, pb, kc, vc)

        a_row = lane_pad(-jnp.exp(gdn_a_log[l].astype(F32)))
        b_row = lane_pad(gdn_dt_bias[l].astype(F32))
        u, w, qh, ktt, intra, eg = _gdn_prep(p, gdn_conv_w[l], a_row, b_row)
        y_gdn = _gdn_scan(p, u, w, qh, ktt, intra, eg, gdn_norm_g[l].reshape(1, LANE))

        y_sb = _stick_breaking(p, pb)

        xs = _out_projection(y_nsa, y_gdn, y_sb, xs, w_out[l].astype(BF16), gate,
                             ln_g[l].reshape(1, d), ln_b[l].reshape(1, d), alpha)
    return xs[None]
```

```python
import functools
import math

import jax
import jax.numpy as jnp
import numpy as np
from jax import lax
from jax.experimental import pallas as pl
from jax.experimental.pallas import tpu as pltpu

F32 = jnp.float32
BF16 = jnp.bfloat16

LANE = 128
HEAD_DIM = 128
ROPE_DIM = HEAD_DIM // 4
ROPE_HALF = ROPE_DIM // 2
ROPE_THETA = 500000.0
Q_BLOCK = 128

NSA_HEADS = 6
NSA_KV_HEADS = 2
NSA_GROUP = NSA_HEADS // NSA_KV_HEADS
CMP_LEN = 32
CMP_STRIDE = 16
SLC_LEN = 64
SLC_TOPK = 16
WINDOW = 512
FORCE_BONUS = 1.0e4
GDN_HEADS = 6
GDN_CONV = 4
SB_HEADS = 4

NSA_W = NSA_HEADS * HEAD_DIM
KV_W = NSA_KV_HEADS * HEAD_DIM
GDN_W = GDN_HEADS * HEAD_DIM
SB_W = SB_HEADS * HEAD_DIM
MIX_W = NSA_W + GDN_W + SB_W

LN_EPS = 1e-5
RMS_EPS = 1e-6
NEG_INF = -1e30
ATTN_SCALE = HEAD_DIM ** -0.5
EXP2_SCALE = ATTN_SCALE * math.log2(math.e)

COL_NQ = 0
COL_NKC = COL_NQ + NSA_W
COL_NVC = COL_NKC + KV_W
COL_NKS = COL_NVC + KV_W
COL_NVS = COL_NKS + KV_W
COL_NKW = COL_NVS + KV_W
COL_NVW = COL_NKW + KV_W
COL_NZ = COL_NVW + KV_W
COL_GQ = COL_NZ + NSA_W
COL_GK = COL_GQ + GDN_W
COL_GV = COL_GK + GDN_W
COL_GZ = COL_GV + GDN_W
COL_SQ = COL_GZ + GDN_W
COL_SK = COL_SQ + SB_W
COL_SV = COL_SK + SB_W
COL_SZ = COL_SV + SB_W
COL_NG = COL_SZ + SB_W
COL_GBA = COL_NG + LANE
P_COLS = COL_GBA + LANE
GA_LANE = GDN_HEADS

GDN_CHUNK = 128
PROJ_TN = 768
PROJ_SUB = 256
SLC_CHUNK = 512
INV_SPLIT_LEVELS = 4
SB_TQ = 512
SB_UNDERFLOW = -104.0
VMEM_LIMIT = 56 * 1024 * 1024


def _cparams(*sem):
    return pltpu.CompilerParams(dimension_semantics=sem, vmem_limit_bytes=VMEM_LIMIT)


def _dot(a, b):
    return jnp.dot(a, b, preferred_element_type=F32)


def _dot_nt(a, b):
    return lax.dot_general(a, b, (((1,), (1,)), ((), ())), preferred_element_type=F32)


def _split3(x):
    hi = x.astype(BF16)
    r1 = x - hi.astype(F32)
    mid = r1.astype(BF16)
    lo = (r1 - mid.astype(F32)).astype(BF16)
    return hi, mid, lo


def _dot_split_rhs(x, m_bf16):
    hi = x.astype(BF16)
    lo = (x - hi.astype(F32)).astype(BF16)
    return _dot(hi, m_bf16) + _dot(lo, m_bf16)


def _dot_exact_lhs(m_bf16, x):
    hi, mid, lo = _split3(x)
    return _dot(m_bf16, hi) + _dot(m_bf16, mid) + _dot(m_bf16, lo)


def _dot_hi(a, b):
    a_hi = a.astype(BF16)
    a_lo = (a - a_hi.astype(F32)).astype(BF16)
    b_hi = b.astype(BF16)
    b_lo = (b - b_hi.astype(F32)).astype(BF16)
    return _dot(a_hi, b_hi) + _dot(a_hi, b_lo) + _dot(a_lo, b_hi)


def _silu(x):
    return x * jax.nn.sigmoid(x)


def _rope_tile(x, cos_t, sin_a, sin_b):
    return (x * cos_t + pltpu.roll(x, LANE - ROPE_HALF, 1) * sin_a
            + pltpu.roll(x, ROPE_HALF, 1) * sin_b)


def _mod_kernel(c_ref, w_ref, b_ref, o_ref):
    c_act = _silu(c_ref[...])
    o_ref[...] = jnp.sum(c_act * w_ref[...], axis=0, keepdims=True) + b_ref[...]


def _modulation(c, w_mod, b_mod):
    depth, d, n = w_mod.shape
    tn = 512
    return pl.pallas_call(
        _mod_kernel,
        grid=(depth, n // tn),
        in_specs=[pl.BlockSpec((d, 1), lambda l, j: (0, 0)),
                  pl.BlockSpec((None, d, tn), lambda l, j: (l, 0, j)),
                  pl.BlockSpec((None, 1, tn), lambda l, j: (l, 0, j))],
        out_specs=pl.BlockSpec((None, 1, tn), lambda l, j: (l, 0, j)),
        out_shape=jax.ShapeDtypeStruct((depth, 1, n), F32),
        compiler_params=_cparams("parallel", "parallel"),
        name="adaln_mod",
    )(c.reshape(d, 1), w_mod, b_mod.reshape(depth, 1, n))


ROPE_TILES = frozenset(range(COL_NQ // LANE, COL_NKC // LANE)) | frozenset(
    range(COL_NKS // LANE, COL_NVS // LANE)) | frozenset(range(COL_NKW // LANE, COL_NVW // LANE))


def _inproj_kernel(x_ref, shift_ref, scale_ref, w_ref, cos_ref, sa_ref, sb_ref, o32_ref, o16_ref,
                   h_ref):
    j = pl.program_id(1)

    def emit(cols, tile):
        o32_ref[:, cols] = tile
        o16_ref[:, cols] = tile.astype(BF16)


    @pl.when(j == 0)
    def _():
        h_ref[...] = (x_ref[...] * (1.0 + scale_ref[...]) + shift_ref[...]).astype(BF16)

    tiles_per_step = PROJ_TN // LANE
    tiles_per_dot = PROJ_SUB // LANE

    def project(step):
        for c0 in range(0, PROJ_TN, PROJ_SUB):
            acc = _dot_nt(h_ref[...], w_ref[c0:c0 + PROJ_SUB, :])
            first_tile = None if step is None else step * tiles_per_step + c0 // LANE
            if first_tile is None or not any(first_tile + t in ROPE_TILES
                                             for t in range(tiles_per_dot)):
                emit(slice(c0, c0 + PROJ_SUB), acc)
                continue
            cos_t, sin_a, sin_b = cos_ref[...], sa_ref[...], sb_ref[...]
            for t in range(tiles_per_dot):
                tile = acc[:, t * LANE:(t + 1) * LANE]
                if first_tile + t in ROPE_TILES:
                    tile = _rope_tile(tile, cos_t, sin_a, sin_b)
                emit(slice(c0 + t * LANE, c0 + (t + 1) * LANE), tile)

    rope_steps = sorted({t // tiles_per_step for t in ROPE_TILES})
    plain = j >= 0
    for step in rope_steps:
        plain = plain & (j != step)
        pl.when(j == step)(functools.partial(project, step))
    pl.when(plain)(functools.partial(project, None))


def _in_projection(x, shift, scale, w_in_b, layer, cos_t, sin_a, sin_b):
    s, d = x.shape
    n = w_in_b.shape[1]
    tm = min(1024, s)
    return pl.pallas_call(
        _inproj_kernel,
        grid=(s // tm, n // PROJ_TN),
        in_specs=[pl.BlockSpec((tm, d), lambda i, j: (i, 0)),
                  pl.BlockSpec((1, d), lambda i, j: (0, 0)),
                  pl.BlockSpec((1, d), lambda i, j: (0, 0)),
                  pl.BlockSpec((None, PROJ_TN, d), lambda i, j: (layer, j, 0)),
                  pl.BlockSpec((tm, LANE), lambda i, j: (i, 0)),
                  pl.BlockSpec((tm, LANE), lambda i, j: (i, 0)),
                  pl.BlockSpec((tm, LANE), lambda i, j: (i, 0))],
        out_specs=[pl.BlockSpec((tm, PROJ_TN), lambda i, j: (i, j)),
                   pl.BlockSpec((tm, PROJ_TN), lambda i, j: (i, j))],
        out_shape=[jax.ShapeDtypeStruct((s, n), F32), jax.ShapeDtypeStruct((s, n), BF16)],
        scratch_shapes=[pltpu.VMEM((tm, d), BF16)],
        compiler_params=_cparams("parallel", "arbitrary"),
        name="in_projection",
    )(x, shift, scale, w_in_b, cos_t, sin_a, sin_b)


def _compress_kernel(k_ref, v_ref, pek_lo, pek_hi, pev_lo, pev_hi, wk_lo, wk_hi, wv_lo, wv_hi,
                     w2k_ref, w2v_ref, cos_ref, sa_ref, sb_ref, kc_ref, vc_ref,
                     ak_ref, bk_ref, av_ref, bv_ref):
    l = pl.program_id(0)

    @pl.when(l == 0)
    def _():
        for r in (ak_ref, bk_ref, av_ref, bv_ref):
            r[...] = jnp.zeros_like(r)

    kx = k_ref[...]
    vx = v_ref[...]
    ak_ref[...] += _dot((kx + pek_lo[...]).astype(BF16), wk_lo[...])
    bk_ref[...] += _dot((kx + pek_hi[...]).astype(BF16), wk_hi[...])
    av_ref[...] += _dot((vx + pev_lo[...]).astype(BF16), wv_lo[...])
    bv_ref[...] += _dot((vx + pev_hi[...]).astype(BF16), wv_hi[...])

    @pl.when(l == CMP_STRIDE - 1)
    def _():
        rows = ak_ref.shape[0]
        hk = ak_ref[...] + pltpu.roll(bk_ref[...], rows - 1, 0)
        hv = av_ref[...] + pltpu.roll(bv_ref[...], rows - 1, 0)
        kc = _dot(_silu(hk).astype(BF16), w2k_ref[...])
        vc = _dot(_silu(hv).astype(BF16), w2v_ref[...])
        cos_t, sin_a, sin_b = cos_ref[...], sa_ref[...], sb_ref[...]
        for g in range(NSA_KV_HEADS):
            sl = slice(g * LANE, (g + 1) * LANE)
            kc_ref[:, sl] = _rope_tile(kc[:, sl], cos_t, sin_a, sin_b).astype(BF16)
        vc_ref[...] = vc.astype(BF16)


def _block_diag2(w):
    z = jnp.zeros_like(w)
    return jnp.concatenate([jnp.concatenate([w, z], axis=-1),
                            jnp.concatenate([z, w], axis=-1)], axis=-2)


def _compress(p, pe_k, pe_v, w1_k, w2_k, w1_v, w2_v, cos_c, sa_c, sb_c):
    s = p.shape[0]
    rows = s // CMP_STRIDE
    assert COL_NVC == COL_NKC + KV_W
    kv_cols = lax.slice(p, (0, COL_NKC), (s, COL_NKC + 2 * KV_W))
    pg = kv_cols.reshape(rows, CMP_STRIDE * 2 * KV_W)
    blocks_per_tok = 2
    kcol, vcol = 0, 1

    def w1_parts(w1):
        w = w1.reshape(CMP_LEN, HEAD_DIM, HEAD_DIM).astype(BF16)
        return _block_diag2(w[:CMP_STRIDE]), _block_diag2(w[CMP_STRIDE:])

    def pe_parts(pe):
        t = jnp.concatenate([pe, pe], axis=-1).reshape(CMP_LEN, 1, KV_W)
        return t[:CMP_STRIDE], t[CMP_STRIDE:]

    wk_lo, wk_hi = w1_parts(w1_k)
    wv_lo, wv_hi = w1_parts(w1_v)
    pek_lo, pek_hi = pe_parts(pe_k)
    pev_lo, pev_hi = pe_parts(pe_v)
    w2k = _block_diag2(w2_k.astype(BF16))
    w2v = _block_diag2(w2_v.astype(BF16))

    tok_spec = lambda col: pl.BlockSpec((rows, KV_W), lambda l: (0, l * blocks_per_tok + col))
    pe_spec = pl.BlockSpec((None, 1, KV_W), lambda l: (l, 0, 0))
    w1_spec = pl.BlockSpec((None, KV_W, KV_W), lambda l: (l, 0, 0))
    full = lambda a: pl.BlockSpec(a.shape, lambda l: (0,) * a.ndim)
    return pl.pallas_call(
        _compress_kernel,
        grid=(CMP_STRIDE,),
        in_specs=[tok_spec(kcol), tok_spec(vcol), pe_spec, pe_spec, pe_spec, pe_spec,
                  w1_spec, w1_spec, w1_spec, w1_spec, full(w2k), full(w2v),
                  full(cos_c), full(sa_c), full(sb_c)],
        out_specs=[pl.BlockSpec((rows, KV_W), lambda l: (0, 0)),
                   pl.BlockSpec((rows, KV_W), lambda l: (0, 0))],
        out_shape=[jax.ShapeDtypeStruct((rows, KV_W), BF16),
                   jax.ShapeDtypeStruct((rows, KV_W), BF16)],
        scratch_shapes=[pltpu.VMEM((rows, KV_W), F32)] * 4,
        compiler_params=_cparams("arbitrary"),
        name="nsa_compress",
    )(pg, pg, pek_lo, pek_hi, pev_lo, pev_hi, wk_lo, wk_hi, wv_lo, wv_hi, w2k, w2v,
      cos_c, sa_c, sb_c)


def _softmax_rows(s, mask):
    s = jnp.where(mask, s, NEG_INF)
    m = jnp.max(s, axis=-1, keepdims=True)
    e = jnp.where(mask, jnp.exp(s - m), 0.0)
    return e / jnp.maximum(jnp.sum(e, axis=-1, keepdims=True), 1e-30)


def _select_top_blocks(imps):
    imp_ts = [imp.T for imp in imps]
    nb = imp_ts[0].shape[0]
    blk = lax.broadcasted_iota(jnp.int32, imp_ts[0].shape, 0).astype(F32)

    def pick_one(_, carry):
        out = []
        for vals, sel in carry:
            best = jnp.max(vals, axis=0, keepdims=True)
            first = jnp.min(jnp.where(vals == best, blk, float(nb)), axis=0, keepdims=True)
            hit = blk == first
            out.append((jnp.where(hit, -jnp.inf, vals), jnp.where(hit, 1.0, sel)))
        return tuple(out)

    done = lax.fori_loop(0, min(SLC_TOPK, nb), pick_one,
                         tuple((imp_t, jnp.zeros_like(imp_t)) for imp_t in imp_ts))
    return [sel_t for _, sel_t in done]


def _nsa_kernel(q_ref, g_ref, z_ref, kc_ref, vc_ref, ov_ref, ks_ref, vst_ref, kw_ref, vw_ref,
                o_ref):
    i = pl.program_id(0)
    tq = q_ref.shape[0]
    n_cmp = kc_ref.shape[0]
    n_slc = ov_ref.shape[1]
    seq = ks_ref.shape[0]
    rows = NSA_GROUP * tq

    t_col = i * tq + lax.broadcasted_iota(jnp.int32, (tq, 1), 0)
    t_rows = jnp.concatenate([t_col] * NSA_GROUP, axis=0)
    gates = jax.nn.sigmoid(g_ref[...])

    cmp_end = lax.broadcasted_iota(jnp.int32, (1, n_cmp), 1) * CMP_STRIDE + (CMP_LEN - 1)
    cmp_mask = cmp_end <= t_rows
    slc_id = lax.broadcasted_iota(jnp.int32, (1, n_slc), 1)
    cur = t_col // SLC_LEN
    future = slc_id * SLC_LEN > t_col
    forced = (slc_id == 0) | (slc_id == cur) | (slc_id == cur - 1)
    t_row = i * tq + lax.broadcasted_iota(jnp.int32, (1, tq), 1)
    future_t = lax.broadcasted_iota(jnp.int32, (n_slc, 1), 0) * SLC_LEN > t_row

    win_len = WINDOW + tq
    win_start = pl.multiple_of(jnp.clip(i * tq - WINDOW, 0, seq - win_len), LANE)
    win_pos = win_start + lax.broadcasted_iota(jnp.int32, (1, win_len), 1)
    dpos = t_rows - win_pos
    win_mask = (dpos >= 0) & (dpos < WINDOW)

    groups = range(NSA_KV_HEADS)
    kv_cols = [slice(g * LANE, (g + 1) * LANE) for g in groups]
    heads_of = [[g * NSA_GROUP + r for r in range(NSA_GROUP)] for g in groups]
    qs = [jnp.concatenate([q_ref[:, h * LANE:(h + 1) * LANE] for h in heads_of[g]], axis=0)
          for g in groups]
    cmp_scores = [_dot_nt(qs[g], kc_ref[:, kv_cols[g]]) * ATTN_SCALE for g in groups]
    pc_bs = [_softmax_rows(cmp_scores[g], cmp_mask).astype(BF16) for g in groups]
    o_cs = [_dot(pc_bs[g], vc_ref[:, kv_cols[g]]) for g in groups]
    imps = []
    for g in groups:
        imp = _dot(pc_bs[g][0:tq], ov_ref[...])
        for r in range(1, NSA_GROUP):
            imp = imp + _dot(pc_bs[g][r * tq:(r + 1) * tq], ov_ref[...])
        imps.append(jnp.where(future, NEG_INF, jnp.where(forced, imp + FORCE_BONUS, imp)))
    sels = [jnp.where(future_t, 0.0, sel_t).astype(BF16) for sel_t in _select_top_blocks(imps)]

    key_blk = lax.broadcasted_iota(jnp.int32, (SLC_CHUNK, n_slc), 0) // SLC_LEN
    blk_id = lax.broadcasted_iota(jnp.int32, (SLC_CHUNK, n_slc), 1)
    key_off = lax.broadcasted_iota(jnp.int32, (SLC_CHUNK, 1), 0)

    def slc_step(c, carry):
        start = pl.multiple_of(c * SLC_CHUNK, SLC_CHUNK)
        causal = (start + key_off) <= t_row
        expand = (blk_id == key_blk + c * (SLC_CHUNK // SLC_LEN)).astype(BF16)
        out = []
        for g in groups:
            m_old, l_old, acc = carry[g]
            k = ks_ref[pl.ds(start, SLC_CHUNK), kv_cols[g]]
            v_t = vst_ref[c, kv_cols[g], :]
            picked = _dot(expand, sels[g])
            bias = jnp.where((picked > 0.5) & causal, 0.0, NEG_INF)
            s = _dot_nt(k, qs[g]) * EXP2_SCALE + jnp.concatenate([bias] * NSA_GROUP, axis=1)
            m_new = jnp.maximum(m_old, jnp.max(s, axis=0, keepdims=True))
            p = jnp.exp2(s - m_new)
            alpha = jnp.exp2(m_old - m_new)
            out.append((m_new, alpha * l_old + jnp.sum(p, axis=0, keepdims=True),
                        alpha * acc + _dot(v_t, p.astype(BF16))))
        return tuple(out)

    init = tuple((jnp.full((1, rows), NEG_INF, F32), jnp.zeros((1, rows), F32),
                  jnp.zeros((HEAD_DIM, rows), F32)) for _ in groups)
    slc = lax.fori_loop(0, (i * tq) // SLC_CHUNK + 1, slc_step, init)

    win_scores = [_dot_nt(qs[g], kw_ref[pl.ds(win_start, win_len), kv_cols[g]]) * ATTN_SCALE
                  for g in groups]
    pw_bs = [_softmax_rows(win_scores[g], win_mask).astype(BF16) for g in groups]
    o_ws = [_dot(pw_bs[g], vw_ref[pl.ds(win_start, win_len), kv_cols[g]]) for g in groups]

    for g in groups:
        _, l_s, acc_s = slc[g]
        o_s_t = acc_s / jnp.maximum(l_s, 1e-30)
        o_s = jnp.concatenate([o_s_t[:, r * tq:(r + 1) * tq].T for r in range(NSA_GROUP)],
                              axis=0)
        o_c, o_w = o_cs[g], o_ws[g]
        for r, h in enumerate(heads_of[g]):
            rs = slice(r * tq, (r + 1) * tq)
            mix = (gates[:, 3 * h:3 * h + 1] * o_c[rs] + gates[:, 3 * h + 1:3 * h + 2] * o_s[rs]
                   + gates[:, 3 * h + 2:3 * h + 3] * o_w[rs])
            hs = slice(h * LANE, (h + 1) * LANE)
            o_ref[:, hs] = (mix * _silu(z_ref[:, hs])).astype(o_ref.dtype)


def _overlap_matrix(n_cmp_rows, n_cmp, n_slc):
    starts = np.arange(n_cmp_rows) * CMP_STRIDE
    ends = starts + CMP_LEN
    blk = np.arange(n_slc)
    ov = np.minimum(ends[:, None], (blk[None, :] + 1) * SLC_LEN) - np.maximum(
        starts[:, None], blk[None, :] * SLC_LEN)
    ov = np.maximum(ov, 0).astype(np.float32) / CMP_LEN
    ov[n_cmp:] = 0.0
    return ov


def _nsa_attention(p, pb, kc, vc):
    s = p.shape[0]
    tq = Q_BLOCK
    n_cmp_rows = kc.shape[0]
    n_cmp = (s - CMP_LEN) // CMP_STRIDE + 1
    n_slc = s // SLC_LEN
    ov = jnp.asarray(_overlap_matrix(n_cmp_rows, n_cmp, n_slc), BF16)
    v_slc = lax.slice(pb, (0, COL_NVS), (s, COL_NVS + KV_W))
    v_slc_t = jnp.transpose(v_slc.reshape(s // SLC_CHUNK, SLC_CHUNK, KV_W), (0, 2, 1))
    full = lambda a: pl.BlockSpec(a.shape, lambda i: (0,) * a.ndim)
    kv_spec = lambda col: pl.BlockSpec((s, KV_W), lambda i: (0, col // KV_W))
    return pl.pallas_call(
        _nsa_kernel,
        grid=(s // tq,),
        in_specs=[pl.BlockSpec((tq, NSA_W), lambda i: (i, COL_NQ // NSA_W)),
                  pl.BlockSpec((tq, LANE), lambda i: (i, COL_NG // LANE)),
                  pl.BlockSpec((tq, NSA_W), lambda i: (i, COL_NZ // NSA_W)),
                  full(kc), full(vc), full(ov),
                  kv_spec(COL_NKS), full(v_slc_t), kv_spec(COL_NKW), kv_spec(COL_NVW)],
        out_specs=pl.BlockSpec((tq, NSA_W), lambda i: (i, 0)),
        out_shape=jax.ShapeDtypeStruct((s, NSA_W), BF16),
        compiler_params=_cparams("parallel"),
        name="nsa_attention",
    )(pb, p, p, kc, vc, ov, pb, v_slc_t, pb, pb)


def _unit_lower_inverses(lows):
    n = lows[0].shape[0]
    eye = (lax.broadcasted_iota(jnp.int32, (n, n), 0)
           == lax.broadcasted_iota(jnp.int32, (n, n), 1)).astype(F32)
    invs = [eye - low for low in lows]
    powers = list(lows)
    for level in range(int(math.log2(n)) - 1):
        mm = _dot_hi if level < INV_SPLIT_LEVELS else (
            lambda a, b: _dot(a.astype(BF16), b.astype(BF16)))
        powers = [mm(pw, pw) for pw in powers]
        invs = [inv + mm(inv, pw) for inv, pw in zip(invs, powers)]
    return invs


def _gdn_prep_kernel(q_ref, k_ref, v_ref, qp_ref, kp_ref, vp_ref, cw_ref, gba_ref, arow_ref,
                     brow_ref, u_ref, w_ref, qh_ref, ktt_ref, intra_ref, eg_ref, xs_ref):
    n = pl.program_id(0)
    c = q_ref.shape[0]
    halo = qp_ref.shape[0]
    keep_halo = (n > 0).astype(F32)

    def conv_silu(cur_ref, prev_ref, col0):
        xs_ref[0:halo, :] = prev_ref[...] * keep_halo
        xs_ref[halo:halo + c, :] = cur_ref[...]
        y = jnp.zeros((c, GDN_W), F32)
        for tap in range(GDN_CONV):
            off = halo - (GDN_CONV - 1) + tap
            y = y + xs_ref[off:off + c, :] * cw_ref[tap:tap + 1, col0:col0 + GDN_W]
        return _silu(y)

    q_all = conv_silu(q_ref, qp_ref, 0)
    k_all = conv_silu(k_ref, kp_ref, GDN_W)
    v_all = conv_silu(v_ref, vp_ref, 2 * GDN_W)

    row = lax.broadcasted_iota(jnp.int32, (c, c), 0)
    col = lax.broadcasted_iota(jnp.int32, (c, c), 1)
    incl = row >= col
    strict = row > col
    tri = incl.astype(BF16)

    gba = gba_ref[...]
    beta_all = jax.nn.sigmoid(gba)
    x = gba + brow_ref[...]
    softplus = jnp.maximum(x, 0.0) + jnp.log1p(jnp.exp(-jnp.abs(x)))
    g_all = arow_ref[...] * softplus
    gc_all = _dot_exact_lhs(tri, g_all)
    eg_ref[...] = jnp.exp(gc_all[c - 1:c, :])

    lows, rhs_u, rhs_w = [], [], []
    for h in range(GDN_HEADS):
        hs = slice(h * LANE, (h + 1) * LANE)
        q = q_all[:, hs]
        k = k_all[:, hs]
        q = q * lax.rsqrt(jnp.sum(q * q, axis=-1, keepdims=True) + RMS_EPS)
        k = k * lax.rsqrt(jnp.sum(k * k, axis=-1, keepdims=True) + RMS_EPS)
        q = q * ATTN_SCALE
        beta = beta_all[:, h:h + 1]
        gc = gc_all[:, GA_LANE + h:GA_LANE + h + 1]
        gc_row = jnp.sum(jnp.where(row == col, gc, 0.0), axis=0, keepdims=True)
        g_last = gc[c - 1:c, :]
        decay = jnp.where(incl, jnp.exp(jnp.where(incl, gc - gc_row, 0.0)), 0.0)
        kb = k * beta
        k_b16 = k.astype(BF16)
        lows.append(jnp.where(strict, _dot_nt(kb.astype(BF16), k_b16) * decay, 0.0))
        rhs_u.append(v_all[:, hs] * beta)
        rhs_w.append(kb * jnp.exp(gc))
        intra_ref[:, hs] = (_dot_nt(q.astype(BF16), k_b16) * decay).astype(BF16)
        qh_ref[:, hs] = (q * jnp.exp(gc)).astype(BF16)
        ktt_ref[:, hs] = (k * jnp.exp(g_last - gc)).T.astype(BF16)

    for h, inv in enumerate(_unit_lower_inverses(lows)):
        hs = slice(h * LANE, (h + 1) * LANE)
        u_ref[:, hs] = _dot_hi(inv, rhs_u[h])
        w_ref[:, hs] = _dot_hi(inv, rhs_w[h]).astype(BF16)


def _gdn_prep(p, conv_w, a_row, b_row):
    s = p.shape[0]
    c = GDN_CHUNK
    halo = 8
    nc = s // c
    cur = lambda col: pl.BlockSpec((c, GDN_W), lambda n: (n, col // GDN_W))
    prev = lambda col: pl.BlockSpec(
        (halo, GDN_W), lambda n: (jnp.maximum(n * (c // halo) - 1, 0), col // GDN_W))
    full = lambda a: pl.BlockSpec(a.shape, lambda n: (0,) * a.ndim)
    tok_out = pl.BlockSpec((c, GDN_W), lambda n: (n, 0))
    return pl.pallas_call(
        _gdn_prep_kernel,
        grid=(nc,),
        in_specs=[cur(COL_GQ), cur(COL_GK), cur(COL_GV), prev(COL_GQ), prev(COL_GK), prev(COL_GV),
                  full(conv_w), pl.BlockSpec((c, LANE), lambda n: (n, COL_GBA // LANE)),
                  full(a_row), full(b_row)],
        out_specs=[tok_out, tok_out, tok_out, tok_out, tok_out,
                   pl.BlockSpec((None, 1, LANE), lambda n: (n, 0, 0))],
        out_shape=[jax.ShapeDtypeStruct((s, GDN_W), F32),
                   jax.ShapeDtypeStruct((s, GDN_W), BF16),
                   jax.ShapeDtypeStruct((s, GDN_W), BF16),
                   jax.ShapeDtypeStruct((s, GDN_W), BF16),
                   jax.ShapeDtypeStruct((s, GDN_W), BF16),
                   jax.ShapeDtypeStruct((nc, 1, LANE), F32)],
        scratch_shapes=[pltpu.VMEM((halo + c, GDN_W), F32)],
        compiler_params=_cparams("parallel"),
        name="gdn_prep",
    )(p, p, p, p, p, p, conv_w, p, a_row, b_row)


def _gdn_scan_kernel(u_ref, w_ref, qh_ref, ktt_ref, intra_ref, eg_ref, z_ref, ng_ref, o_ref,
                     state_ref):
    @pl.when(pl.program_id(0) == 0)
    def _():
        state_ref[...] = jnp.zeros_like(state_ref)

    eg = eg_ref[...]
    heads = range(GDN_HEADS)
    cols = [slice(h * LANE, (h + 1) * LANE) for h in heads]
    states = [state_ref[h] for h in heads]
    states_b = [s.astype(BF16) for s in states]
    v_new_b = [(u_ref[:, cols[h]] - _dot(w_ref[:, cols[h]], states_b[h])).astype(BF16)
               for h in heads]
    outs = [_dot(qh_ref[:, cols[h]], states_b[h]) + _dot(intra_ref[:, cols[h]], v_new_b[h])
            for h in heads]
    for h in heads:
        state_ref[h] = (states[h] * eg[:, GA_LANE + h:GA_LANE + h + 1]
                        + _dot(ktt_ref[:, cols[h]], v_new_b[h]))
    for h in heads:
        o = outs[h]
        o = o * lax.rsqrt(jnp.mean(o * o, axis=-1, keepdims=True) + RMS_EPS) * ng_ref[...]
        o_ref[:, cols[h]] = (o * _silu(z_ref[:, cols[h]])).astype(o_ref.dtype)


def _gdn_scan(p, u, w, qh, ktt, intra, eg, norm_g):
    s = p.shape[0]
    c = GDN_CHUNK
    tok = pl.BlockSpec((c, GDN_W), lambda n: (n, 0))
    return pl.pallas_call(
        _gdn_scan_kernel,
        grid=(s // c,),
        in_specs=[tok, tok, tok, tok, tok,
                  pl.BlockSpec((None, 1, LANE), lambda n: (n, 0, 0)),
                  pl.BlockSpec((c, GDN_W), lambda n: (n, COL_GZ // GDN_W)),
                  pl.BlockSpec((1, LANE), lambda n: (0, 0))],
        out_specs=tok,
        out_shape=jax.ShapeDtypeStruct((s, GDN_W), BF16),
        scratch_shapes=[pltpu.VMEM((GDN_HEADS, HEAD_DIM, HEAD_DIM), F32)],
        compiler_params=_cparams("arbitrary"),
        name="gdn_scan",
    )(u, w, qh, ktt, intra, eg, p, norm_g)


def _sb_kernel(q_ref, k_ref, v_ref, z_ref, o_ref):
    i = pl.program_id(1)
    tq = q_ref.shape[0]
    q = q_ref[...]
    t_col = i * tq + lax.broadcasted_iota(jnp.int32, (tq, 1), 0)
    later = (lax.broadcasted_iota(jnp.int32, (LANE, LANE), 0)
             > lax.broadcasted_iota(jnp.int32, (LANE, LANE), 1)).astype(BF16)

    def tile_step(m, carry, first_row=None):
        tail, acc = carry
        on_diagonal = first_row is not None
        rows = slice(first_row, None)
        start = pl.multiple_of(m * LANE, LANE)
        k = k_ref[pl.ds(start, LANE), :]
        v = v_ref[pl.ds(start, LANE), :]
        z = _dot_nt(q[rows], k) * ATTN_SCALE
        log_beta = jnp.minimum(z, 0.0) - jnp.log(1.0 + jnp.exp(-jnp.abs(z)))
        log_1m = log_beta - z
        if on_diagonal:
            past = (start + lax.broadcasted_iota(jnp.int32, (1, LANE), 1)) < t_col[rows]
            log_1m = jnp.where(past, log_1m, 0.0)
        a = jnp.exp(log_beta + _dot_split_rhs(log_1m, later) + tail[rows])
        if on_diagonal:
            a = jnp.where(past, a, 0.0)
        acc_rows = acc[rows] + _dot(a.astype(BF16), v)
        tail_rows = tail[rows] + jnp.sum(log_1m, axis=-1, keepdims=True)
        if not first_row:
            return tail_rows, acc_rows
        return (jnp.concatenate([tail[:first_row], tail_rows], axis=0),
                jnp.concatenate([acc[:first_row], acc_rows], axis=0))

    diag_tiles = tq // LANE
    top = (i + 1) * diag_tiles - 1
    carry = (jnp.zeros((tq, 1), F32), jnp.zeros((tq, HEAD_DIM), F32))
    for d in range(diag_tiles):
        carry = tile_step(top - d, carry, first_row=(diag_tiles - 1 - d) * LANE)
    tail, acc = carry

    def live(state):
        m, tail, _ = state
        return (m >= 0) & (jnp.max(tail) >= SB_UNDERFLOW)

    def older(state):
        m, tail, acc = state
        tail, acc = tile_step(m, (tail, acc))
        return m - 1, tail, acc

    _, _, acc = lax.while_loop(live, older, (i * diag_tiles - 1, tail, acc))
    o_ref[...] = (acc * _silu(z_ref[...])).astype(o_ref.dtype)


def _stick_breaking(p, pb):
    s = p.shape[0]
    tq = min(SB_TQ, s)
    return pl.pallas_call(
        _sb_kernel,
        grid=(SB_HEADS, s // tq),
        in_specs=[pl.BlockSpec((tq, LANE), lambda h, i: (i, COL_SQ // LANE + h)),
                  pl.BlockSpec((s, LANE), lambda h, i: (0, COL_SK // LANE + h)),
                  pl.BlockSpec((s, LANE), lambda h, i: (0, COL_SV // LANE + h)),
                  pl.BlockSpec((tq, LANE), lambda h, i: (i, COL_SZ // LANE + h))],
        out_specs=pl.BlockSpec((tq, LANE), lambda h, i: (i, h)),
        out_shape=jax.ShapeDtypeStruct((s, SB_W), BF16),
        compiler_params=_cparams("parallel", "parallel"),
        name="stick_breaking",
    )(pb, pb, pb, p)


def _outproj_kernel(alpha, yn_ref, yg_ref, ys_ref, x_ref, w_ref, gate_ref, g_ref, b_ref, o_ref):
    y = (_dot(yn_ref[...], w_ref[0:NSA_W, :])
         + _dot(yg_ref[...], w_ref[NSA_W:NSA_W + GDN_W, :])
         + _dot(ys_ref[...], w_ref[NSA_W + GDN_W:MIX_W, :]))
    r = alpha * x_ref[...] + (1.0 + gate_ref[...]) * y
    mu = jnp.mean(r, axis=-1, keepdims=True)
    var = jnp.mean(jnp.square(r - mu), axis=-1, keepdims=True)
    o_ref[...] = (r - mu) * lax.rsqrt(var + LN_EPS) * g_ref[...] + b_ref[...]


def _out_projection(y_nsa, y_gdn, y_sb, x, w_out_b, gate, ln_g, ln_b, alpha):
    s, d = x.shape
    tm = min(256, s)
    row = lambda i: (i, 0)
    const = lambda i: (0, 0)
    return pl.pallas_call(
        functools.partial(_outproj_kernel, alpha),
        grid=(s // tm,),
        in_specs=[pl.BlockSpec((tm, NSA_W), row), pl.BlockSpec((tm, GDN_W), row),
                  pl.BlockSpec((tm, SB_W), row), pl.BlockSpec((tm, d), row),
                  pl.BlockSpec((MIX_W, d), const), pl.BlockSpec((1, d), const),
                  pl.BlockSpec((1, d), const), pl.BlockSpec((1, d), const)],
        out_specs=pl.BlockSpec((tm, d), row),
        out_shape=jax.ShapeDtypeStruct((s, d), F32),
        compiler_params=_cparams("parallel"),
        name="out_projection",
    )(y_nsa, y_gdn, y_sb, x, w_out_b, gate, ln_g, ln_b)


def _segment_tables():
    nsa_qkv_w = NSA_W + 6 * KV_W
    segments = []
    o = 0
    for width, dst in ((nsa_qkv_w, COL_NQ), (3 * NSA_HEADS, COL_NG), (NSA_W, COL_NZ),
                       (3 * GDN_W, COL_GQ), (2 * GDN_HEADS, COL_GBA), (GDN_W, COL_GZ),
                       (4 * SB_W, COL_SQ)):
        segments.append((o, width, dst))
        o += width
    src = np.zeros(P_COLS // LANE, np.int32)
    shift = np.zeros_like(src)
    width_tab = np.zeros_like(src)
    for start, width, dst in segments:
        for off in range(0, width, LANE):
            j = (dst + off) // LANE
            src[j] = (start + off) // LANE
            shift[j] = (start + off) % LANE
            width_tab[j] = min(LANE, width - off)
    return src, shift, width_tab, o


def _wprep_kernel(row_ref, width_ref, w_ref, o_ref):
    j = pl.program_id(0)
    valid = lax.broadcasted_iota(jnp.int32, (LANE, 1), 0) < width_ref[j]
    for l in range(o_ref.shape[0]):
        o_ref[l] = jnp.where(valid, w_ref[:, l, :], 0.0).astype(BF16)


def _aligned_w_in(w_in):
    depth, d, n = w_in.shape
    src, shift, width_tab, total = _segment_tables()
    assert total == n
    row_start = src * LANE + shift
    w_t = jnp.transpose(w_in, (2, 0, 1))
    grid_spec = pltpu.PrefetchScalarGridSpec(
        num_scalar_prefetch=2,
        grid=(P_COLS // LANE,),
        in_specs=[pl.BlockSpec((pl.Element(LANE), pl.Element(depth), pl.Element(d)),
                               lambda j, rows, _: (rows[j], 0, 0))],
        out_specs=pl.BlockSpec((depth, LANE, d), lambda j, *_: (0, j, 0)))
    return pl.pallas_call(
        _wprep_kernel,
        grid_spec=grid_spec,
        out_shape=jax.ShapeDtypeStruct((depth, P_COLS, d), BF16),
        compiler_params=_cparams("parallel"),
        name="w_in_layout",
    )(jnp.asarray(row_start), jnp.asarray(width_tab), w_t)


def _rope_tables(pos):
    inv = ROPE_THETA ** (-jnp.arange(ROPE_HALF, dtype=F32) * 2.0 / ROPE_DIM)
    ang = pos.astype(F32)[:, None] * inv
    cos, sin = jnp.cos(ang), jnp.sin(ang)
    n = pos.shape[0]
    rest = HEAD_DIM - ROPE_DIM
    cos_t = jnp.concatenate([cos, cos, jnp.ones((n, rest), F32)], axis=1)
    sin_a = jnp.concatenate([-sin, jnp.zeros((n, HEAD_DIM - ROPE_HALF), F32)], axis=1)
    sin_b = jnp.concatenate([jnp.zeros((n, ROPE_HALF), F32), sin, jnp.zeros((n, rest), F32)], axis=1)
    return cos_t, sin_a, sin_b


def kernel(x, c, positions, w_mod, b_mod, w_in, w_out, ln_g, ln_b, cmp_pe_k, cmp_pe_v, cmp_w1_k, cmp_w2_k, cmp_w1_v, cmp_w2_v, gdn_conv_w, gdn_a_log, gdn_dt_bias, gdn_norm_g):
    b, s, d = x.shape
    assert b == 1 and s % (8 * Q_BLOCK) == 0 and s >= WINDOW + Q_BLOCK
    depth = w_mod.shape[0]
    alpha = (2 * depth) ** 0.25
    xs = x[0]
    pos = positions[0]

    cos_t, sin_a, sin_b = _rope_tables(pos)
    n_rows = s // CMP_STRIDE
    cmp_end = jnp.minimum(jnp.arange(n_rows) * CMP_STRIDE + CMP_LEN - 1, s - 1)
    cos_c, sa_c, sb_c = _rope_tables(pos[cmp_end])

    mod = _modulation(c, w_mod, b_mod)
    lane_pad = lambda v: jnp.pad(v, (GA_LANE, LANE - GA_LANE - v.shape[0])).reshape(1, LANE)

    w_in_b = _aligned_w_in(w_in)

    for l in range(depth):
        shift, scale, gate = (mod[l, :, k * d:(k + 1) * d] for k in range(3))
        p, pb = _in_projection(xs, shift, scale, w_in_b, l, cos_t, sin_a, sin_b)

        kc, vc = _compress(p, cmp_pe_k[l], cmp_pe_v[l], cmp_w1_k[l], cmp_w2_k[l],
                           cmp_w1_v[l], cmp_w2_v[l], cos_c, sa_c, sb_c)
        y_nsa = _nsa_attention(p, pb, kc, vc)

        a_row = lane_pad(-jnp.exp(gdn_a_log[l].astype(F32)))
        b_row = lane_pad(gdn_dt_bias[l].astype(F32))
        u, w, qh, ktt, intra, eg = _gdn_prep(p, gdn_conv_w[l], a_row, b_row)
        y_gdn = _gdn_scan(p, u, w, qh, ktt, intra, eg, gdn_norm_g[l].reshape(1, LANE))

        y_sb = _stick_breaking(p, pb)

        xs = _out_projection(y_nsa, y_gdn, y_sb, xs, w_out[l].astype(BF16), gate,
                             ln_g[l].reshape(1, d), ln_b[l].reshape(1, d), alpha)
    return xs[None]
```

```python
import functools
import math

import jax
import jax.numpy as jnp
import numpy as np
from jax import lax
from jax.experimental import pallas as pl
from jax.experimental.pallas import tpu as pltpu

F32 = jnp.float32
BF16 = jnp.bfloat16

LANE = 128
HEAD_DIM = 128
ROPE_DIM = HEAD_DIM // 4
ROPE_HALF = ROPE_DIM // 2
ROPE_THETA = 500000.0
Q_BLOCK = 128

NSA_HEADS = 6
NSA_KV_HEADS = 2
NSA_GROUP = NSA_HEADS // NSA_KV_HEADS
CMP_LEN = 32
CMP_STRIDE = 16
SLC_LEN = 64
SLC_TOPK = 16
WINDOW = 512
FORCE_BONUS = 1.0e4
GDN_HEADS = 6
GDN_CONV = 4
SB_HEADS = 4

NSA_W = NSA_HEADS * HEAD_DIM
KV_W = NSA_KV_HEADS * HEAD_DIM
GDN_W = GDN_HEADS * HEAD_DIM
SB_W = SB_HEADS * HEAD_DIM
MIX_W = NSA_W + GDN_W + SB_W

LN_EPS = 1e-5
RMS_EPS = 1e-6
NEG_INF = -1e30
ATTN_SCALE = HEAD_DIM ** -0.5
EXP2_SCALE = ATTN_SCALE * math.log2(math.e)

COL_NQ = 0
COL_NKC = COL_NQ + NSA_W
COL_NVC = COL_NKC + KV_W
COL_NKS = COL_NVC + KV_W
COL_NVS = COL_NKS + KV_W
COL_NKW = COL_NVS + KV_W
COL_NVW = COL_NKW + KV_W
COL_NZ = COL_NVW + KV_W
COL_GQ = COL_NZ + NSA_W
COL_GK = COL_GQ + GDN_W
COL_GV = COL_GK + GDN_W
COL_GZ = COL_GV + GDN_W
COL_SQ = COL_GZ + GDN_W
COL_SK = COL_SQ + SB_W
COL_SV = COL_SK + SB_W
COL_SZ = COL_SV + SB_W
COL_NG = COL_SZ + SB_W
COL_GBA = COL_NG + LANE
P_COLS = COL_GBA + LANE
GA_LANE = GDN_HEADS

GDN_CHUNK = 128
PROJ_TN = 768
PROJ_SUB = 256
SLC_CHUNK = 1024
INV_SPLIT_LEVELS = 4
SB_TQ = 512
SB_UNDERFLOW = -104.0
VMEM_LIMIT = 56 * 1024 * 1024


def _cparams(*sem):
    return pltpu.CompilerParams(dimension_semantics=sem, vmem_limit_bytes=VMEM_LIMIT)


def _dot(a, b):
    return jnp.dot(a, b, preferred_element_type=F32)


def _dot_nt(a, b):
    return lax.dot_general(a, b, (((1,), (1,)), ((), ())), preferred_element_type=F32)


def _split3(x):
    hi = x.astype(BF16)
    r1 = x - hi.astype(F32)
    mid = r1.astype(BF16)
    lo = (r1 - mid.astype(F32)).astype(BF16)
    return hi, mid, lo


def _dot_split_rhs(x, m_bf16):
    hi = x.astype(BF16)
    lo = (x - hi.astype(F32)).astype(BF16)
    return _dot(hi, m_bf16) + _dot(lo, m_bf16)


def _dot_exact_lhs(m_bf16, x):
    hi, mid, lo = _split3(x)
    return _dot(m_bf16, hi) + _dot(m_bf16, mid) + _dot(m_bf16, lo)


def _dot_hi(a, b):
    a_hi = a.astype(BF16)
    a_lo = (a - a_hi.astype(F32)).astype(BF16)
    b_hi = b.astype(BF16)
    b_lo = (b - b_hi.astype(F32)).astype(BF16)
    return _dot(a_hi, b_hi) + _dot(a_hi, b_lo) + _dot(a_lo, b_hi)


def _silu(x):
    return x * jax.nn.sigmoid(x)


def _rope_tile(x, cos_t, sin_a, sin_b):
    return (x * cos_t + pltpu.roll(x, LANE - ROPE_HALF, 1) * sin_a
            + pltpu.roll(x, ROPE_HALF, 1) * sin_b)


def _mod_kernel(c_ref, w_ref, b_ref, o_ref):
    c_act = _silu(c_ref[...])
    o_ref[...] = jnp.sum(c_act * w_ref[...], axis=0, keepdims=True) + b_ref[...]


def _modulation(c, w_mod, b_mod):
    depth, d, n = w_mod.shape
    tn = 512
    return pl.pallas_call(
        _mod_kernel,
        grid=(depth, n // tn),
        in_specs=[pl.BlockSpec((d, 1), lambda l, j: (0, 0)),
                  pl.BlockSpec((None, d, tn), lambda l, j: (l, 0, j)),
                  pl.BlockSpec((None, 1, tn), lambda l, j: (l, 0, j))],
        out_specs=pl.BlockSpec((None, 1, tn), lambda l, j: (l, 0, j)),
        out_shape=jax.ShapeDtypeStruct((depth, 1, n), F32),
        compiler_params=_cparams("parallel", "parallel"),
        name="adaln_mod",
    )(c.reshape(d, 1), w_mod, b_mod.reshape(depth, 1, n))


ROPE_TILES = frozenset(range(COL_NQ // LANE, COL_NKC // LANE)) | frozenset(
    range(COL_NKS // LANE, COL_NVS // LANE)) | frozenset(range(COL_NKW // LANE, COL_NVW // LANE))


def _inproj_kernel(x_ref, shift_ref, scale_ref, w_ref, cos_ref, sa_ref, sb_ref, o32_ref, o16_ref,
                   h_ref):
    j = pl.program_id(1)

    def emit(cols, tile):
        o32_ref[:, cols] = tile
        o16_ref[:, cols] = tile.astype(BF16)


    @pl.when(j == 0)
    def _():
        h_ref[...] = (x_ref[...] * (1.0 + scale_ref[...]) + shift_ref[...]).astype(BF16)

    tiles_per_step = PROJ_TN // LANE
    tiles_per_dot = PROJ_SUB // LANE

    def project(step):
        for c0 in range(0, PROJ_TN, PROJ_SUB):
            acc = _dot_nt(h_ref[...], w_ref[c0:c0 + PROJ_SUB, :])
            first_tile = None if step is None else step * tiles_per_step + c0 // LANE
            if first_tile is None or not any(first_tile + t in ROPE_TILES
                                             for t in range(tiles_per_dot)):
                emit(slice(c0, c0 + PROJ_SUB), acc)
                continue
            cos_t, sin_a, sin_b = cos_ref[...], sa_ref[...], sb_ref[...]
            for t in range(tiles_per_dot):
                tile = acc[:, t * LANE:(t + 1) * LANE]
                if first_tile + t in ROPE_TILES:
                    tile = _rope_tile(tile, cos_t, sin_a, sin_b)
                emit(slice(c0 + t * LANE, c0 + (t + 1) * LANE), tile)

    rope_steps = sorted({t // tiles_per_step for t in ROPE_TILES})
    plain = j >= 0
    for step in rope_steps:
        plain = plain & (j != step)
        pl.when(j == step)(functools.partial(project, step))
    pl.when(plain)(functools.partial(project, None))


def _in_projection(x, shift, scale, w_in_b, layer, cos_t, sin_a, sin_b):
    s, d = x.shape
    n = w_in_b.shape[1]
    tm = min(1024, s)
    return pl.pallas_call(
        _inproj_kernel,
        grid=(s // tm, n // PROJ_TN),
        in_specs=[pl.BlockSpec((tm, d), lambda i, j: (i, 0)),
                  pl.BlockSpec((1, d), lambda i, j: (0, 0)),
                  pl.BlockSpec((1, d), lambda i, j: (0, 0)),
                  pl.BlockSpec((None, PROJ_TN, d), lambda i, j: (layer, j, 0)),
                  pl.BlockSpec((tm, LANE), lambda i, j: (i, 0)),
                  pl.BlockSpec((tm, LANE), lambda i, j: (i, 0)),
                  pl.BlockSpec((tm, LANE), lambda i, j: (i, 0))],
        out_specs=[pl.BlockSpec((tm, PROJ_TN), lambda i, j: (i, j)),
                   pl.BlockSpec((tm, PROJ_TN), lambda i, j: (i, j))],
        out_shape=[jax.ShapeDtypeStruct((s, n), F32), jax.ShapeDtypeStruct((s, n), BF16)],
        scratch_shapes=[pltpu.VMEM((tm, d), BF16)],
        compiler_params=_cparams("parallel", "arbitrary"),
        name="in_projection",
    )(x, shift, scale, w_in_b, cos_t, sin_a, sin_b)


def _compress_kernel(k_ref, v_ref, pek_lo, pek_hi, pev_lo, pev_hi, wk_lo, wk_hi, wv_lo, wv_hi,
                     w2k_ref, w2v_ref, cos_ref, sa_ref, sb_ref, kc_ref, vc_ref,
                     ak_ref, bk_ref, av_ref, bv_ref):
    l = pl.program_id(0)

    @pl.when(l == 0)
    def _():
        for r in (ak_ref, bk_ref, av_ref, bv_ref):
            r[...] = jnp.zeros_like(r)

    kx = k_ref[...]
    vx = v_ref[...]
    ak_ref[...] += _dot((kx + pek_lo[...]).astype(BF16), wk_lo[...])
    bk_ref[...] += _dot((kx + pek_hi[...]).astype(BF16), wk_hi[...])
    av_ref[...] += _dot((vx + pev_lo[...]).astype(BF16), wv_lo[...])
    bv_ref[...] += _dot((vx + pev_hi[...]).astype(BF16), wv_hi[...])

    @pl.when(l == CMP_STRIDE - 1)
    def _():
        rows = ak_ref.shape[0]
        hk = ak_ref[...] + pltpu.roll(bk_ref[...], rows - 1, 0)
        hv = av_ref[...] + pltpu.roll(bv_ref[...], rows - 1, 0)
        kc = _dot(_silu(hk).astype(BF16), w2k_ref[...])
        vc = _dot(_silu(hv).astype(BF16), w2v_ref[...])
        cos_t, sin_a, sin_b = cos_ref[...], sa_ref[...], sb_ref[...]
        for g in range(NSA_KV_HEADS):
            sl = slice(g * LANE, (g + 1) * LANE)
            kc_ref[:, sl] = _rope_tile(kc[:, sl], cos_t, sin_a, sin_b).astype(BF16)
        vc_ref[...] = vc.astype(BF16)


def _block_diag2(w):
    z = jnp.zeros_like(w)
    return jnp.concatenate([jnp.concatenate([w, z], axis=-1),
                            jnp.concatenate([z, w], axis=-1)], axis=-2)


def _compress(p, pe_k, pe_v, w1_k, w2_k, w1_v, w2_v, cos_c, sa_c, sb_c):
    s = p.shape[0]
    rows = s // CMP_STRIDE
    assert COL_NVC == COL_NKC + KV_W
    kv_cols = lax.slice(p, (0, COL_NKC), (s, COL_NKC + 2 * KV_W))
    pg = kv_cols.reshape(rows, CMP_STRIDE * 2 * KV_W)
    blocks_per_tok = 2
    kcol, vcol = 0, 1

    def w1_parts(w1):
        w = w1.reshape(CMP_LEN, HEAD_DIM, HEAD_DIM).astype(BF16)
        return _block_diag2(w[:CMP_STRIDE]), _block_diag2(w[CMP_STRIDE:])

    def pe_parts(pe):
        t = jnp.concatenate([pe, pe], axis=-1).reshape(CMP_LEN, 1, KV_W)
        return t[:CMP_STRIDE], t[CMP_STRIDE:]

    wk_lo, wk_hi = w1_parts(w1_k)
    wv_lo, wv_hi = w1_parts(w1_v)
    pek_lo, pek_hi = pe_parts(pe_k)
    pev_lo, pev_hi = pe_parts(pe_v)
    w2k = _block_diag2(w2_k.astype(BF16))
    w2v = _block_diag2(w2_v.astype(BF16))

    tok_spec = lambda col: pl.BlockSpec((rows, KV_W), lambda l: (0, l * blocks_per_tok + col))
    pe_spec = pl.BlockSpec((None, 1, KV_W), lambda l: (l, 0, 0))
    w1_spec = pl.BlockSpec((None, KV_W, KV_W), lambda l: (l, 0, 0))
    full = lambda a: pl.BlockSpec(a.shape, lambda l: (0,) * a.ndim)
    return pl.pallas_call(
        _compress_kernel,
        grid=(CMP_STRIDE,),
        in_specs=[tok_spec(kcol), tok_spec(vcol), pe_spec, pe_spec, pe_spec, pe_spec,
                  w1_spec, w1_spec, w1_spec, w1_spec, full(w2k), full(w2v),
                  full(cos_c), full(sa_c), full(sb_c)],
        out_specs=[pl.BlockSpec((rows, KV_W), lambda l: (0, 0)),
                   pl.BlockSpec((rows, KV_W), lambda l: (0, 0))],
        out_shape=[jax.ShapeDtypeStruct((rows, KV_W), BF16),
                   jax.ShapeDtypeStruct((rows, KV_W), BF16)],
        scratch_shapes=[pltpu.VMEM((rows, KV_W), F32)] * 4,
        compiler_params=_cparams("arbitrary"),
        name="nsa_compress",
    )(pg, pg, pek_lo, pek_hi, pev_lo, pev_hi, wk_lo, wk_hi, wv_lo, wv_hi, w2k, w2v,
      cos_c, sa_c, sb_c)


def _softmax_rows(s, mask):
    s = jnp.where(mask, s, NEG_INF)
    m = jnp.max(s, axis=-1, keepdims=True)
    e = jnp.where(mask, jnp.exp(s - m), 0.0)
    return e / jnp.maximum(jnp.sum(e, axis=-1, keepdims=True), 1e-30)


def _select_top_blocks(imps):
    imp_ts = [imp.T for imp in imps]
    nb = imp_ts[0].shape[0]
    blk = lax.broadcasted_iota(jnp.int32, imp_ts[0].shape, 0).astype(F32)

    def pick_one(_, carry):
        out = []
        for vals, sel in carry:
            best = jnp.max(vals, axis=0, keepdims=True)
            first = jnp.min(jnp.where(vals == best, blk, float(nb)), axis=0, keepdims=True)
            hit = blk == first
            out.append((jnp.where(hit, -jnp.inf, vals), jnp.where(hit, 1.0, sel)))
        return tuple(out)

    done = lax.fori_loop(0, min(SLC_TOPK, nb), pick_one,
                         tuple((imp_t, jnp.zeros_like(imp_t)) for imp_t in imp_ts))
    return [sel_t for _, sel_t in done]


def _nsa_kernel(q_ref, g_ref, z_ref, kc_ref, vc_ref, ov_ref, ks_ref, vst_ref, kw_ref, vw_ref,
                o_ref):
    i = pl.program_id(0)
    tq = q_ref.shape[0]
    n_cmp = kc_ref.shape[0]
    n_slc = ov_ref.shape[1]
    seq = ks_ref.shape[0]
    rows = NSA_GROUP * tq

    t_col = i * tq + lax.broadcasted_iota(jnp.int32, (tq, 1), 0)
    t_rows = jnp.concatenate([t_col] * NSA_GROUP, axis=0)
    gates = jax.nn.sigmoid(g_ref[...])

    cmp_end = lax.broadcasted_iota(jnp.int32, (1, n_cmp), 1) * CMP_STRIDE + (CMP_LEN - 1)
    cmp_mask = cmp_end <= t_rows
    slc_id = lax.broadcasted_iota(jnp.int32, (1, n_slc), 1)
    cur = t_col // SLC_LEN
    future = slc_id * SLC_LEN > t_col
    forced = (slc_id == 0) | (slc_id == cur) | (slc_id == cur - 1)
    t_row = i * tq + lax.broadcasted_iota(jnp.int32, (1, tq), 1)
    future_t = lax.broadcasted_iota(jnp.int32, (n_slc, 1), 0) * SLC_LEN > t_row

    win_len = WINDOW + tq
    win_start = pl.multiple_of(jnp.clip(i * tq - WINDOW, 0, seq - win_len), LANE)
    win_pos = win_start + lax.broadcasted_iota(jnp.int32, (1, win_len), 1)
    dpos = t_rows - win_pos
    win_mask = (dpos >= 0) & (dpos < WINDOW)

    groups = range(NSA_KV_HEADS)
    kv_cols = [slice(g * LANE, (g + 1) * LANE) for g in groups]
    heads_of = [[g * NSA_GROUP + r for r in range(NSA_GROUP)] for g in groups]
    qs = [jnp.concatenate([q_ref[:, h * LANE:(h + 1) * LANE] for h in heads_of[g]], axis=0)
          for g in groups]
    cmp_scores = [_dot_nt(qs[g], kc_ref[:, kv_cols[g]]) * ATTN_SCALE for g in groups]
    pc_bs = [_softmax_rows(cmp_scores[g], cmp_mask).astype(BF16) for g in groups]
    o_cs = [_dot(pc_bs[g], vc_ref[:, kv_cols[g]]) for g in groups]
    imps = []
    for g in groups:
        imp = _dot(pc_bs[g][0:tq], ov_ref[...])
        for r in range(1, NSA_GROUP):
            imp = imp + _dot(pc_bs[g][r * tq:(r + 1) * tq], ov_ref[...])
        imps.append(jnp.where(future, NEG_INF, jnp.where(forced, imp + FORCE_BONUS, imp)))
    sels = [jnp.where(future_t, 0.0, sel_t).astype(BF16) for sel_t in _select_top_blocks(imps)]

    key_blk = lax.broadcasted_iota(jnp.int32, (SLC_CHUNK, n_slc), 0) // SLC_LEN
    blk_id = lax.broadcasted_iota(jnp.int32, (SLC_CHUNK, n_slc), 1)
    key_off = lax.broadcasted_iota(jnp.int32, (SLC_CHUNK, 1), 0)

    def slc_step(c, carry):
        start = pl.multiple_of(c * SLC_CHUNK, SLC_CHUNK)
        causal = (start + key_off) <= t_row
        expand = (blk_id == key_blk + c * (SLC_CHUNK // SLC_LEN)).astype(BF16)
        out = []
        for g in groups:
            m_old, l_old, acc = carry[g]
            k = ks_ref[pl.ds(start, SLC_CHUNK), kv_cols[g]]
            v_t = vst_ref[c, kv_cols[g], :]
            picked = _dot(expand, sels[g])
            bias = jnp.where((picked > 0.5) & causal, 0.0, NEG_INF)
            s = _dot_nt(k, qs[g]) * EXP2_SCALE + jnp.concatenate([bias] * NSA_GROUP, axis=1)
            m_new = jnp.maximum(m_old, jnp.max(s, axis=0, keepdims=True))
            p = jnp.exp2(s - m_new)
            alpha = jnp.exp2(m_old - m_new)
            out.append((m_new, alpha * l_old + jnp.sum(p, axis=0, keepdims=True),
                        alpha * acc + _dot(v_t, p.astype(BF16))))
        return tuple(out)

    init = tuple((jnp.full((1, rows), NEG_INF, F32), jnp.zeros((1, rows), F32),
                  jnp.zeros((HEAD_DIM, rows), F32)) for _ in groups)
    slc = lax.fori_loop(0, (i * tq) // SLC_CHUNK + 1, slc_step, init)

    win_scores = [_dot_nt(qs[g], kw_ref[pl.ds(win_start, win_len), kv_cols[g]]) * ATTN_SCALE
                  for g in groups]
    pw_bs = [_softmax_rows(win_scores[g], win_mask).astype(BF16) for g in groups]
    o_ws = [_dot(pw_bs[g], vw_ref[pl.ds(win_start, win_len), kv_cols[g]]) for g in groups]

    for g in groups:
        _, l_s, acc_s = slc[g]
        o_s_t = acc_s / jnp.maximum(l_s, 1e-30)
        o_s = jnp.concatenate([o_s_t[:, r * tq:(r + 1) * tq].T for r in range(NSA_GROUP)],
                              axis=0)
        o_c, o_w = o_cs[g], o_ws[g]
        for r, h in enumerate(heads_of[g]):
            rs = slice(r * tq, (r + 1) * tq)
            mix = (gates[:, 3 * h:3 * h + 1] * o_c[rs] + gates[:, 3 * h + 1:3 * h + 2] * o_s[rs]
                   + gates[:, 3 * h + 2:3 * h + 3] * o_w[rs])
            hs = slice(h * LANE, (h + 1) * LANE)
            o_ref[:, hs] = (mix * _silu(z_ref[:, hs])).astype(o_ref.dtype)


def _overlap_matrix(n_cmp_rows, n_cmp, n_slc):
    starts = np.arange(n_cmp_rows) * CMP_STRIDE
    ends = starts + CMP_LEN
    blk = np.arange(n_slc)
    ov = np.minimum(ends[:, None], (blk[None, :] + 1) * SLC_LEN) - np.maximum(
        starts[:, None], blk[None, :] * SLC_LEN)
    ov = np.maximum(ov, 0).astype(np.float32) / CMP_LEN
    ov[n_cmp:] = 0.0
    return ov


def _nsa_attention(p, pb, kc, vc):
    s = p.shape[0]
    tq = Q_BLOCK
    n_cmp_rows = kc.shape[0]
    n_cmp = (s - CMP_LEN) // CMP_STRIDE + 1
    n_slc = s // SLC_LEN
    ov = jnp.asarray(_overlap_matrix(n_cmp_rows, n_cmp, n_slc), BF16)
    v_slc = lax.slice(pb, (0, COL_NVS), (s, COL_NVS + KV_W))
    v_slc_t = jnp.transpose(v_slc.reshape(s // SLC_CHUNK, SLC_CHUNK, KV_W), (0, 2, 1))
    full = lambda a: pl.BlockSpec(a.shape, lambda i: (0,) * a.ndim)
    kv_spec = lambda col: pl.BlockSpec((s, KV_W), lambda i: (0, col // KV_W))
    return pl.pallas_call(
        _nsa_kernel,
        grid=(s // tq,),
        in_specs=[pl.BlockSpec((tq, NSA_W), lambda i: (i, COL_NQ // NSA_W)),
                  pl.BlockSpec((tq, LANE), lambda i: (i, COL_NG // LANE)),
                  pl.BlockSpec((tq, NSA_W), lambda i: (i, COL_NZ // NSA_W)),
                  full(kc), full(vc), full(ov),
                  kv_spec(COL_NKS), full(v_slc_t), kv_spec(COL_NKW), kv_spec(COL_NVW)],
        out_specs=pl.BlockSpec((tq, NSA_W), lambda i: (i, 0)),
        out_shape=jax.ShapeDtypeStruct((s, NSA_W), BF16),
        compiler_params=_cparams("parallel"),
        name="nsa_attention",
    )(pb, p, p, kc, vc, ov, pb, v_slc_t, pb, pb)


def _unit_lower_inverses(lows):
    n = lows[0].shape[0]
    eye = (lax.broadcasted_iota(jnp.int32, (n, n), 0)
           == lax.broadcasted_iota(jnp.int32, (n, n), 1)).astype(F32)
    invs = [eye - low for low in lows]
    powers = list(lows)
    for level in range(int(math.log2(n)) - 1):
        mm = _dot_hi if level < INV_SPLIT_LEVELS else (
            lambda a, b: _dot(a.astype(BF16), b.astype(BF16)))
        powers = [mm(pw, pw) for pw in powers]
        invs = [inv + mm(inv, pw) for inv, pw in zip(invs, powers)]
    return invs


def _gdn_prep_kernel(q_ref, k_ref, v_ref, qp_ref, kp_ref, vp_ref, cw_ref, gba_ref, arow_ref,
                     brow_ref, u_ref, w_ref, qh_ref, ktt_ref, intra_ref, eg_ref, xs_ref):
    n = pl.program_id(0)
    c = q_ref.shape[0]
    halo = qp_ref.shape[0]
    keep_halo = (n > 0).astype(F32)

    def conv_silu(cur_ref, prev_ref, col0):
        xs_ref[0:halo, :] = prev_ref[...] * keep_halo
        xs_ref[halo:halo + c, :] = cur_ref[...]
        y = jnp.zeros((c, GDN_W), F32)
        for tap in range(GDN_CONV):
            off = halo - (GDN_CONV - 1) + tap
            y = y + xs_ref[off:off + c, :] * cw_ref[tap:tap + 1, col0:col0 + GDN_W]
        return _silu(y)

    q_all = conv_silu(q_ref, qp_ref, 0)
    k_all = conv_silu(k_ref, kp_ref, GDN_W)
    v_all = conv_silu(v_ref, vp_ref, 2 * GDN_W)

    row = lax.broadcasted_iota(jnp.int32, (c, c), 0)
    col = lax.broadcasted_iota(jnp.int32, (c, c), 1)
    incl = row >= col
    strict = row > col
    tri = incl.astype(BF16)

    gba = gba_ref[...]
    beta_all = jax.nn.sigmoid(gba)
    x = gba + brow_ref[...]
    softplus = jnp.maximum(x, 0.0) + jnp.log1p(jnp.exp(-jnp.abs(x)))
    g_all = arow_ref[...] * softplus
    gc_all = _dot_exact_lhs(tri, g_all)
    eg_ref[...] = jnp.exp(gc_all[c - 1:c, :])

    lows, rhs_u, rhs_w = [], [], []
    for h in range(GDN_HEADS):
        hs = slice(h * LANE, (h + 1) * LANE)
        q = q_all[:, hs]
        k = k_all[:, hs]
        q = q * lax.rsqrt(jnp.sum(q * q, axis=-1, keepdims=True) + RMS_EPS)
        k = k * lax.rsqrt(jnp.sum(k * k, axis=-1, keepdims=True) + RMS_EPS)
        q = q * ATTN_SCALE
        beta = beta_all[:, h:h + 1]
        gc = gc_all[:, GA_LANE + h:GA_LANE + h + 1]
        gc_row = jnp.sum(jnp.where(row == col, gc, 0.0), axis=0, keepdims=True)
        g_last = gc[c - 1:c, :]
        decay = jnp.where(incl, jnp.exp(jnp.where(incl, gc - gc_row, 0.0)), 0.0)
        kb = k * beta
        k_b16 = k.astype(BF16)
        lows.append(jnp.where(strict, _dot_nt(kb.astype(BF16), k_b16) * decay, 0.0))
        rhs_u.append(v_all[:, hs] * beta)
        rhs_w.append(kb * jnp.exp(gc))
        intra_ref[:, hs] = (_dot_nt(q.astype(BF16), k_b16) * decay).astype(BF16)
        qh_ref[:, hs] = (q * jnp.exp(gc)).astype(BF16)
        ktt_ref[:, hs] = (k * jnp.exp(g_last - gc)).T.astype(BF16)

    for h, inv in enumerate(_unit_lower_inverses(lows)):
        hs = slice(h * LANE, (h + 1) * LANE)
        u_ref[:, hs] = _dot_hi(inv, rhs_u[h])
        w_ref[:, hs] = _dot_hi(inv, rhs_w[h]).astype(BF16)


def _gdn_prep(p, conv_w, a_row, b_row):
    s = p.shape[0]
    c = GDN_CHUNK
    halo = 8
    nc = s // c
    cur = lambda col: pl.BlockSpec((c, GDN_W), lambda n: (n, col // GDN_W))
    prev = lambda col: pl.BlockSpec(
        (halo, GDN_W), lambda n: (jnp.maximum(n * (c // halo) - 1, 0), col // GDN_W))
    full = lambda a: pl.BlockSpec(a.shape, lambda n: (0,) * a.ndim)
    tok_out = pl.BlockSpec((c, GDN_W), lambda n: (n, 0))
    return pl.pallas_call(
        _gdn_prep_kernel,
        grid=(nc,),
        in_specs=[cur(COL_GQ), cur(COL_GK), cur(COL_GV), prev(COL_GQ), prev(COL_GK), prev(COL_GV),
                  full(conv_w), pl.BlockSpec((c, LANE), lambda n: (n, COL_GBA // LANE)),
                  full(a_row), full(b_row)],
        out_specs=[tok_out, tok_out, tok_out, tok_out, tok_out,
                   pl.BlockSpec((None, 1, LANE), lambda n: (n, 0, 0))],
        out_shape=[jax.ShapeDtypeStruct((s, GDN_W), F32),
                   jax.ShapeDtypeStruct((s, GDN_W), BF16),
                   jax.ShapeDtypeStruct((s, GDN_W), BF16),
                   jax.ShapeDtypeStruct((s, GDN_W), BF16),
                   jax.ShapeDtypeStruct((s, GDN_W), BF16),
                   jax.ShapeDtypeStruct((nc, 1, LANE), F32)],
        scratch_shapes=[pltpu.VMEM((halo + c, GDN_W), F32)],
        compiler_params=_cparams("parallel"),
        name="gdn_prep",
    )(p, p, p, p, p, p, conv_w, p, a_row, b_row)


def _gdn_scan_kernel(u_ref, w_ref, qh_ref, ktt_ref, intra_ref, eg_ref, z_ref, ng_ref, o_ref,
                     state_ref):
    @pl.when(pl.program_id(0) == 0)
    def _():
        state_ref[...] = jnp.zeros_like(state_ref)

    eg = eg_ref[...]
    heads = range(GDN_HEADS)
    cols = [slice(h * LANE, (h + 1) * LANE) for h in heads]
    states = [state_ref[h] for h in heads]
    states_b = [s.astype(BF16) for s in states]
    v_new_b = [(u_ref[:, cols[h]] - _dot(w_ref[:, cols[h]], states_b[h])).astype(BF16)
               for h in heads]
    outs = [_dot(qh_ref[:, cols[h]], states_b[h]) + _dot(intra_ref[:, cols[h]], v_new_b[h])
            for h in heads]
    for h in heads:
        state_ref[h] = (states[h] * eg[:, GA_LANE + h:GA_LANE + h + 1]
                        + _dot(ktt_ref[:, cols[h]], v_new_b[h]))
    for h in heads:
        o = outs[h]
        o = o * lax.rsqrt(jnp.mean(o * o, axis=-1, keepdims=True) + RMS_EPS) * ng_ref[...]
        o_ref[:, cols[h]] = (o * _silu(z_ref[:, cols[h]])).astype(o_ref.dtype)


def _gdn_scan(p, u, w, qh, ktt, intra, eg, norm_g):
    s = p.shape[0]
    c = GDN_CHUNK
    tok = pl.BlockSpec((c, GDN_W), lambda n: (n, 0))
    return pl.pallas_call(
        _gdn_scan_kernel,
        grid=(s // c,),
        in_specs=[tok, tok, tok, tok, tok,
                  pl.BlockSpec((None, 1, LANE), lambda n: (n, 0, 0)),
                  pl.BlockSpec((c, GDN_W), lambda n: (n, COL_GZ // GDN_W)),
                  pl.BlockSpec((1, LANE), lambda n: (0, 0))],
        out_specs=tok,
        out_shape=jax.ShapeDtypeStruct((s, GDN_W), BF16),
        scratch_shapes=[pltpu.VMEM((GDN_HEADS, HEAD_DIM, HEAD_DIM), F32)],
        compiler_params=_cparams("arbitrary"),
        name="gdn_scan",
    )(u, w, qh, ktt, intra, eg, p, norm_g)


def _sb_kernel(q_ref, k_ref, v_ref, z_ref, o_ref):
    i = pl.program_id(1)
    tq = q_ref.shape[0]
    q = q_ref[...]
    t_col = i * tq + lax.broadcasted_iota(jnp.int32, (tq, 1), 0)
    later = (lax.broadcasted_iota(jnp.int32, (LANE, LANE), 0)
             > lax.broadcasted_iota(jnp.int32, (LANE, LANE), 1)).astype(BF16)

    def tile_step(m, carry, first_row=None):
        tail, acc = carry
        on_diagonal = first_row is not None
        rows = slice(first_row, None)
        start = pl.multiple_of(m * LANE, LANE)
        k = k_ref[pl.ds(start, LANE), :]
        v = v_ref[pl.ds(start, LANE), :]
        z = _dot_nt(q[rows], k) * ATTN_SCALE
        log_beta = jnp.minimum(z, 0.0) - jnp.log(1.0 + jnp.exp(-jnp.abs(z)))
        log_1m = log_beta - z
        if on_diagonal:
            past = (start + lax.broadcasted_iota(jnp.int32, (1, LANE), 1)) < t_col[rows]
            log_1m = jnp.where(past, log_1m, 0.0)
        a = jnp.exp(log_beta + _dot_split_rhs(log_1m, later) + tail[rows])
        if on_diagonal:
            a = jnp.where(past, a, 0.0)
        acc_rows = acc[rows] + _dot(a.astype(BF16), v)
        tail_rows = tail[rows] + jnp.sum(log_1m, axis=-1, keepdims=True)
        if not first_row:
            return tail_rows, acc_rows
        return (jnp.concatenate([tail[:first_row], tail_rows], axis=0),
                jnp.concatenate([acc[:first_row], acc_rows], axis=0))

    diag_tiles = tq // LANE
    top = (i + 1) * diag_tiles - 1
    carry = (jnp.zeros((tq, 1), F32), jnp.zeros((tq, HEAD_DIM), F32))
    for d in range(diag_tiles):
        carry = tile_step(top - d, carry, first_row=(diag_tiles - 1 - d) * LANE)
    tail, acc = carry

    def live(state):
        m, tail, _ = state
        return (m >= 0) & (jnp.max(tail) >= SB_UNDERFLOW)

    def older(state):
        m, tail, acc = state
        tail, acc = tile_step(m, (tail, acc))
        return m - 1, tail, acc

    _, _, acc = lax.while_loop(live, older, (i * diag_tiles - 1, tail, acc))
    o_ref[...] = (acc * _silu(z_ref[...])).astype(o_ref.dtype)


def _stick_breaking(p, pb):
    s = p.shape[0]
    tq = min(SB_TQ, s)
    return pl.pallas_call(
        _sb_kernel,
        grid=(SB_HEADS, s // tq),
        in_specs=[pl.BlockSpec((tq, LANE), lambda h, i: (i, COL_SQ // LANE + h)),
                  pl.BlockSpec((s, LANE), lambda h, i: (0, COL_SK // LANE + h)),
                  pl.BlockSpec((s, LANE), lambda h, i: (0, COL_SV // LANE + h)),
                  pl.BlockSpec((tq, LANE), lambda h, i: (i, COL_SZ // LANE + h))],
        out_specs=pl.BlockSpec((tq, LANE), lambda h, i: (i, h)),
        out_shape=jax.ShapeDtypeStruct((s, SB_W), BF16),
        compiler_params=_cparams("parallel", "parallel"),
        name="stick_breaking",
    )(pb, pb, pb, p)


def _outproj_kernel(alpha, yn_ref, yg_ref, ys_ref, x_ref, w_ref, gate_ref, g_ref, b_ref, o_ref):
    y = (_dot(yn_ref[...], w_ref[0:NSA_W, :])
         + _dot(yg_ref[...], w_ref[NSA_W:NSA_W + GDN_W, :])
         + _dot(ys_ref[...], w_ref[NSA_W + GDN_W:MIX_W, :]))
    r = alpha * x_ref[...] + (1.0 + gate_ref[...]) * y
    mu = jnp.mean(r, axis=-1, keepdims=True)
    var = jnp.mean(jnp.square(r - mu), axis=-1, keepdims=True)
    o_ref[...] = (r - mu) * lax.rsqrt(var + LN_EPS) * g_ref[...] + b_ref[...]


def _out_projection(y_nsa, y_gdn, y_sb, x, w_out_b, gate, ln_g, ln_b, alpha):
    s, d = x.shape
    tm = min(256, s)
    row = lambda i: (i, 0)
    const = lambda i: (0, 0)
    return pl.pallas_call(
        functools.partial(_outproj_kernel, alpha),
        grid=(s // tm,),
        in_specs=[pl.BlockSpec((tm, NSA_W), row), pl.BlockSpec((tm, GDN_W), row),
                  pl.BlockSpec((tm, SB_W), row), pl.BlockSpec((tm, d), row),
                  pl.BlockSpec((MIX_W, d), const), pl.BlockSpec((1, d), const),
                  pl.BlockSpec((1, d), const), pl.BlockSpec((1, d), const)],
        out_specs=pl.BlockSpec((tm, d), row),
        out_shape=jax.ShapeDtypeStruct((s, d), F32),
        compiler_params=_cparams("parallel"),
        name="out_projection",
    )(y_nsa, y_gdn, y_sb, x, w_out_b, gate, ln_g, ln_b)


def _segment_tables():
    nsa_qkv_w = NSA_W + 6 * KV_W
    segments = []
    o = 0
    for width, dst in ((nsa_qkv_w, COL_NQ), (3 * NSA_HEADS, COL_NG), (NSA_W, COL_NZ),
                       (3 * GDN_W, COL_GQ), (2 * GDN_HEADS, COL_GBA), (GDN_W, COL_GZ),
                       (4 * SB_W, COL_SQ)):
        segments.append((o, width, dst))
        o += width
    src = np.zeros(P_COLS // LANE, np.int32)
    shift = np.zeros_like(src)
    width_tab = np.zeros_like(src)
    for start, width, dst in segments:
        for off in range(0, width, LANE):
            j = (dst + off) // LANE
            src[j] = (start + off) // LANE
            shift[j] = (start + off) % LANE
            width_tab[j] = min(LANE, width - off)
    return src, shift, width_tab, o


def _wprep_kernel(row_ref, width_ref, w_ref, o_ref):
    j = pl.program_id(0)
    valid = lax.broadcasted_iota(jnp.int32, (LANE, 1), 0) < width_ref[j]
    for l in range(o_ref.shape[0]):
        o_ref[l] = jnp.where(valid, w_ref[:, l, :], 0.0).astype(BF16)


def _aligned_w_in(w_in):
    depth, d, n = w_in.shape
    src, shift, width_tab, total = _segment_tables()
    assert total == n
    row_start = src * LANE + shift
    w_t = jnp.transpose(w_in, (2, 0, 1))
    grid_spec = pltpu.PrefetchScalarGridSpec(
        num_scalar_prefetch=2,
        grid=(P_COLS // LANE,),
        in_specs=[pl.BlockSpec((pl.Element(LANE), pl.Element(depth), pl.Element(d)),
                               lambda j, rows, _: (rows[j], 0, 0))],
        out_specs=pl.BlockSpec((depth, LANE, d), lambda j, *_: (0, j, 0)))
    return pl.pallas_call(
        _wprep_kernel,
        grid_spec=grid_spec,
        out_shape=jax.ShapeDtypeStruct((depth, P_COLS, d), BF16),
        compiler_params=_cparams("parallel"),
        name="w_in_layout",
    )(jnp.asarray(row_start), jnp.asarray(width_tab), w_t)


def _rope_tables(pos):
    inv = ROPE_THETA ** (-jnp.arange(ROPE_HALF, dtype=F32) * 2.0 / ROPE_DIM)
    ang = pos.astype(F32)[:, None] * inv
    cos, sin = jnp.cos(ang), jnp.sin(ang)
    n = pos.shape[0]
    rest = HEAD_DIM - ROPE_DIM
    cos_t = jnp.concatenate([cos, cos, jnp.ones((n, rest), F32)], axis=1)
    sin_a = jnp.concatenate([-sin, jnp.zeros((n, HEAD_DIM - ROPE_HALF), F32)], axis=1)
    sin_b = jnp.concatenate([jnp.zeros((n, ROPE_HALF), F32), sin, jnp.zeros((n, rest), F32)], axis=1)
    return cos_t, sin_a, sin_b


def kernel(x, c, positions, w_mod, b_mod, w_in, w_out, ln_g, ln_b, cmp_pe_k, cmp_pe_v, cmp_w1_k, cmp_w2_k, cmp_w1_v, cmp_w2_v, gdn_conv_w, gdn_a_log, gdn_dt_bias, gdn_norm_g):
    b, s, d = x.shape
    assert b == 1 and s % (8 * Q_BLOCK) == 0 and s >= WINDOW + Q_BLOCK
    depth = w_mod.shape[0]
    alpha = (2 * depth) ** 0.25
    xs = x[0]
    pos = positions[0]

    cos_t, sin_a, sin_b = _rope_tables(pos)
    n_rows = s // CMP_STRIDE
    cmp_end = jnp.minimum(jnp.arange(n_rows) * CMP_STRIDE + CMP_LEN - 1, s - 1)
    cos_c, sa_c, sb_c = _rope_tables(pos[cmp_end])

    mod = _modulation(c, w_mod, b_mod)
    lane_pad = lambda v: jnp.pad(v, (GA_LANE, LANE - GA_LANE - v.shape[0])).reshape(1, LANE)

    w_in_b = _aligned_w_in(w_in)

    for l in range(depth):
        shift, scale, gate = (mod[l, :, k * d:(k + 1) * d] for k in range(3))
        p, pb = _in_projection(xs, shift, scale, w_in_b, l, cos_t, sin_a, sin_b)

        kc, vc = _compress(p, cmp_pe_k[l], cmp_pe_v[l], cmp_w1_k[l], cmp_w2_k[l],
                           cmp_w1_v[l], cmp_w2_v[l], cos_c, sa_c, sb_c)
        y_nsa = _nsa_attention(p, pb, kc, vc)

        a_row = lane_pad(-jnp.exp(gdn_a_log[l].astype(F32)))
        b_row = lane_pad(gdn_dt_bias[l].astype(F32))
        u, w, qh, ktt, intra, eg = _gdn_prep(p, gdn_conv_w[l], a_row, b_row)
        y_gdn = _gdn_scan(p, u, w, qh, ktt, intra, eg, gdn_norm_g[l].reshape(1, LANE))

        y_sb = _stick_breaking(p, pb)

        xs = _out_projection(y_nsa, y_gdn, y_sb, xs, w_out[l].astype(BF16), gate,
                             ln_g[l].reshape(1, d), ln_b[l].reshape(1, d), alpha)
    return xs[None]
```
